```python
import math
import jax, jax.numpy as jnp
from jax import lax
import numpy as np

D_MODEL = 1024
BATCH = 16
SEQ = 4096
DEPTH = 1

SSM_EXPAND = 2
SSM_D_INNER = SSM_EXPAND * D_MODEL
SSM_HEAD_DIM = 64
SSM_N_HEADS = SSM_D_INNER // SSM_HEAD_DIM
SSM_N_GROUPS = 4
SSM_D_STATE = 128
SSM_CONV = 4
SSM_CHUNK = 128
SSM_CONV_DIM = SSM_D_INNER + 2 * SSM_N_GROUPS * SSM_D_STATE

ATT_HEAD_DIM = 128
ATT_HEADS_PER_GROUP = 4
ATT_PATTERNS = ((128, 1), (512, 4), (2048, 16))
ATT_N_HEADS = ATT_HEADS_PER_GROUP * len(ATT_PATTERNS)
ATT_QKV_DIM = 3 * ATT_N_HEADS * ATT_HEAD_DIM
ATT_OUT_DIM = ATT_HEADS_PER_GROUP * ATT_HEAD_DIM
ATT_BLOCK = 128
ROPE_THETA = 10000.0

N_BRANCH = 2
D_FF = -(-8 * D_MODEL // (3 * 256)) * 256
IN_PROJ_SIZES = (SSM_D_INNER, SSM_CONV_DIM, SSM_N_HEADS, ATT_QKV_DIM, N_BRANCH * D_MODEL)
IN_PROJ_DIM = sum(IN_PROJ_SIZES)
EPS = 1e-6

kernel_name = "hybrid_ssd_dilated_swa_block"


def rmsnorm(x, g):
    xf = x.astype(jnp.float32)
    y = xf * lax.rsqrt(jnp.mean(xf * xf, axis=-1, keepdims=True) + EPS)
    return (y * g.astype(jnp.float32)).astype(x.dtype)


def rope(t, pos):
    half = t.shape[-1] // 2
    inv = ROPE_THETA ** (-jnp.arange(half, dtype=jnp.float32) / half)
    ang = pos.astype(jnp.float32)[:, None] * inv[None, :]
    cos = jnp.cos(ang)[None, :, None, :]
    sin = jnp.sin(ang)[None, :, None, :]
    t1, t2 = t[..., :half], t[..., half:]
    return jnp.concatenate([t1 * cos - t2 * sin, t2 * cos + t1 * sin], axis=-1)


def segsum(a):
    T = a.shape[-1]
    xx = jnp.broadcast_to(a[..., :, None], a.shape + (T,))
    cs = jnp.cumsum(jnp.where(jnp.tril(jnp.ones((T, T), bool), -1), xx, 0.0), axis=-2)
    return jnp.where(jnp.tril(jnp.ones((T, T), bool)), cs, -jnp.inf)


def causal_depthwise_conv(u, w, b):
    K, C = w.shape
    out = lax.conv_general_dilated(u, w[:, None, :], window_strides=(1,), padding=[(K - 1, 0)],
                                   dimension_numbers=('NWC', 'WIO', 'NWC'), feature_group_count=C)
    return out + b


def ssd_chunked(xs, dt, A, Bm, Cm):
    b, S, H, P = xs.shape
    G, N = Bm.shape[-2:]
    J = H // G
    Q = SSM_CHUNK
    nc = S // Q
    xdt = (xs * dt[..., None]).reshape(b, nc, Q, G, J, P)
    a = (dt * A).reshape(b, nc, Q, G, J).transpose(0, 1, 3, 4, 2)
    a_cs = jnp.cumsum(a, axis=-1)
    Br = Bm.reshape(b, nc, Q, G, N)
    Cr = Cm.reshape(b, nc, Q, G, N)
    tri = jnp.tril(jnp.ones((Q, Q), bool))
    Lmat = jnp.exp(jnp.where(tri, a_cs[..., :, None] - a_cs[..., None, :], -jnp.inf))
    CB = jnp.einsum('bclgn,bcsgn->bcgls', Cr, Br)
    y_diag = jnp.einsum('bcgjls,bcsgjp->bclgjp', CB[:, :, :, None] * Lmat, xdt)
    decay_states = jnp.exp(a_cs[..., -1:] - a_cs)
    states = jnp.einsum('bclgn,bcgjl,bclgjp->bcgjpn', Br, decay_states, xdt)
    chunk_tot = jnp.pad(a_cs[..., -1].transpose(0, 2, 3, 1), ((0, 0), (0, 0), (0, 0), (1, 0)))
    decay_chunk = jnp.exp(segsum(chunk_tot))
    states = jnp.concatenate([jnp.zeros_like(states[:, :1]), states], axis=1)
    new_states = jnp.einsum('bgjzc,bcgjpn->bzgjpn', decay_chunk, states)
    prev_states = new_states[:, :-1]
    y_off = jnp.einsum('bclgn,bcgjpn,bcgjl->bclgjp', Cr, prev_states, jnp.exp(a_cs))
    return (y_diag + y_off).reshape(b, S, H, P)


def mamba2_branch(z, xBC, dt_raw, conv_w, conv_b, dt_bias, a_log, d_skip, ssm_norm):
    b, S, _ = z.shape
    xBC = jax.nn.silu(causal_depthwise_conv(xBC, conv_w, conv_b)).astype(jnp.float32)
    gn = SSM_N_GROUPS * SSM_D_STATE
    xs = xBC[..., :SSM_D_INNER].reshape(b, S, SSM_N_HEADS, SSM_HEAD_DIM)
    Bm = xBC[..., SSM_D_INNER:SSM_D_INNER + gn].reshape(b, S, SSM_N_GROUPS, SSM_D_STATE)
    Cm = xBC[..., SSM_D_INNER + gn:].reshape(b, S, SSM_N_GROUPS, SSM_D_STATE)
    dt = jax.nn.softplus(dt_raw.astype(jnp.float32) + dt_bias.astype(jnp.float32))
    A = -jnp.exp(a_log.astype(jnp.float32))
    y = ssd_chunked(xs, dt, A, Bm, Cm) + d_skip.astype(jnp.float32)[:, None] * xs
    y = y.reshape(b, S, SSM_D_INNER) * jax.nn.silu(z.astype(jnp.float32))
    yg = y.reshape(b, S, SSM_N_GROUPS, SSM_D_INNER // SSM_N_GROUPS)
    yg = yg * lax.rsqrt(jnp.mean(yg * yg, axis=-1, keepdims=True) + EPS)
    y = yg.reshape(b, S, SSM_D_INNER) * ssm_norm.astype(jnp.float32)
    return y.astype(z.dtype)


def dilated_window_group(q, k, v, window, dilation):
    b, S, h, d = q.shape
    r = dilation
    w_sub = window // r
    L = S // r
    nb = -(-L // ATT_BLOCK)
    Lp = nb * ATT_BLOCK

    def to_blocks(t):
        t = t.reshape(b, L, r, h, d).transpose(0, 2, 1, 3, 4)
        t = jnp.pad(t, ((0, 0), (0, 0), (0, Lp - L), (0, 0), (0, 0)))
        return t.reshape(b, r, nb, ATT_BLOCK, h, d)

    def with_prev(t):
        prev = jnp.pad(t[:, :, :-1], ((0, 0), (0, 0), (1, 0), (0, 0), (0, 0), (0, 0)))
        return jnp.concatenate([prev, t], axis=3)

    qb = to_blocks(q)
    kk = with_prev(to_blocks(k))
    vv = with_prev(to_blocks(v))
    s = jnp.einsum('brnqhd,brnkhd->brnhqk', qb, kk) * (d ** -0.5)
    qi = jnp.arange(ATT_BLOCK)[:, None]
    kj = jnp.arange(2 * ATT_BLOCK)[None, :]
    dist = qi + ATT_BLOCK - kj
    band = (dist >= 0) & (dist <= w_sub)
    has_prev = (jnp.arange(nb) > 0)[:, None, None] | (kj >= ATT_BLOCK)[None]
    mask = band[None] & has_prev
    s = jnp.where(mask[None, None, :, None], s, -jnp.inf)
    m = jnp.max(s, axis=-1, keepdims=True)
    p = jnp.exp(s - m)
    den = jnp.sum(p, axis=-1, keepdims=True)
    o = jnp.einsum('brnhqk,brnkhd->brnqhd', p / den, vv)
    lse = (m + jnp.log(den))[..., 0].transpose(0, 1, 2, 4, 3)
    o = o.reshape(b, r, Lp, h, d)[:, :, :L].transpose(0, 2, 1, 3, 4).reshape(b, S, h, d)
    lse = lse.reshape(b, r, Lp, h)[:, :, :L].transpose(0, 2, 1, 3).reshape(b, S, h)
    return o, lse


def dilated_attention_branch(q, k, v):
    outs, lses = [], []
    for gi, (window, dilation) in enumerate(ATT_PATTERNS):
        sl = slice(gi * ATT_HEADS_PER_GROUP, (gi + 1) * ATT_HEADS_PER_GROUP)
        o, lse = dilated_window_group(q[:, :, sl], k[:, :, sl], v[:, :, sl], window, dilation)
        outs.append(o)
        lses.append(lse)
    o = jnp.stack(outs, axis=0)
    wts = jax.nn.softmax(jnp.stack(lses, axis=0), axis=0)
    return jnp.sum(wts[..., None] * o, axis=0)


def setup_inputs(seed: int = 0) -> dict:
    key = jax.random.key(seed)
    ks = jax.random.split(key, 20)
    f32 = jnp.float32

    def nrm(k, shape, scale):
        return jax.random.normal(k, shape, f32) * scale

    dt0 = jnp.exp(jax.random.uniform(ks[6], (DEPTH, SSM_N_HEADS), f32,
                                     minval=math.log(1e-3), maxval=math.log(1e-1)))
    return {
        "x": nrm(ks[0], (BATCH, SEQ, D_MODEL), 1.0),
        "norm_mix": 1.0 + nrm(ks[1], (DEPTH, D_MODEL), 0.05),
        "w_in": nrm(ks[2], (DEPTH, D_MODEL, IN_PROJ_DIM), D_MODEL ** -0.5),
        "b_gate": nrm(ks[3], (DEPTH, N_BRANCH * D_MODEL), 0.01),
        "conv_w": nrm(ks[4], (DEPTH, SSM_CONV, SSM_CONV_DIM), SSM_CONV ** -0.5),
        "conv_b": nrm(ks[5], (DEPTH, SSM_CONV_DIM), 0.01),
        "dt_bias": dt0 + jnp.log(-jnp.expm1(-dt0)),
        "a_log": jnp.log(jax.random.uniform(ks[7], (DEPTH, SSM_N_HEADS), f32, minval=1.0, maxval=16.0)),
        "d_skip": 1.0 + nrm(ks[8], (DEPTH, SSM_N_HEADS), 0.1),
        "ssm_norm": 1.0 + nrm(ks[9], (DEPTH, SSM_D_INNER), 0.05),
        "w_ssm_out": nrm(ks[10], (DEPTH, SSM_D_INNER, D_MODEL), SSM_D_INNER ** -0.5),
        "w_att_out": nrm(ks[11], (DEPTH, ATT_OUT_DIM, D_MODEL), ATT_OUT_DIM ** -0.5),
        "w_mix_out": nrm(ks[12], (DEPTH, D_MODEL, D_MODEL), D_MODEL ** -0.5),
        "norm_ffn": 1.0 + nrm(ks[13], (DEPTH, D_MODEL), 0.05),
        "w_ffn_gate": nrm(ks[14], (DEPTH, D_MODEL, D_FF), D_MODEL ** -0.5),
        "w_ffn_up": nrm(ks[15], (DEPTH, D_MODEL, D_FF), D_MODEL ** -0.5),
        "w_ffn_down": nrm(ks[16], (DEPTH, D_FF, D_MODEL), D_FF ** -0.5),
        "norm_final": 1.0 + nrm(ks[17], (D_MODEL,), 0.05),
    }


def reference(x, norm_mix, w_in, b_gate, conv_w, conv_b, dt_bias, a_log, d_skip, ssm_norm,
              w_ssm_out, w_att_out, w_mix_out, norm_ffn, w_ffn_gate, w_ffn_up, w_ffn_down, norm_final):
    b, S, _ = x.shape
    pos = jnp.arange(S)
    offs = [0]
    for sz in IN_PROJ_SIZES[:-1]:
        offs.append(offs[-1] + sz)
    for l in range(DEPTH):
        h = rmsnorm(x, norm_mix[l])
        proj = h @ w_in[l]
        z, xBC, dt_raw, qkv, gate_logits = jnp.split(proj, offs[1:], axis=-1)

        y_ssm = mamba2_branch(z, xBC, dt_raw, conv_w[l], conv_b[l], dt_bias[l], a_log[l],
                              d_skip[l], ssm_norm[l]) @ w_ssm_out[l]

        qkv = qkv.astype(jnp.float32).reshape(b, S, 3, ATT_N_HEADS, ATT_HEAD_DIM)
        q = rope(qkv[:, :, 0], pos)
        k = rope(qkv[:, :, 1], pos)
        v = qkv[:, :, 2]
        y_att = dilated_attention_branch(q, k, v).reshape(b, S, ATT_OUT_DIM).astype(x.dtype) @ w_att_out[l]

        gates = jax.nn.sigmoid((gate_logits + b_gate[l]).astype(jnp.float32))
        gates = gates.reshape(b, S, N_BRANCH, D_MODEL).astype(x.dtype)
        mixed = gates[:, :, 0] * y_ssm + gates[:, :, 1] * y_att
        x = x + mixed @ w_mix_out[l]

        h = rmsnorm(x, norm_ffn[l])
        x = x + (jax.nn.silu(h @ w_ffn_gate[l]) * (h @ w_ffn_up[l])) @ w_ffn_down[l]
    return rmsnorm(x, norm_final)
```

```python
import functools
import math

import jax
import jax.numpy as jnp
from jax import lax
from jax.experimental import pallas as pl
from jax.experimental.pallas import tpu as pltpu

D_MODEL = 1024
SSM_D_INNER = 2048
SSM_HEAD_DIM = 64
SSM_N_HEADS = 32
SSM_N_GROUPS = 4
SSM_HEADS_PER_GROUP = SSM_N_HEADS // SSM_N_GROUPS
SSM_GROUP_DIM = SSM_D_INNER // SSM_N_GROUPS
SSM_D_STATE = 128
SSM_CONV = 4
SSM_CHUNK = 128
SSM_BC_DIM = SSM_N_GROUPS * SSM_D_STATE
SSM_CONV_DIM = SSM_D_INNER + 2 * SSM_BC_DIM

ATT_HEAD_DIM = 128
ATT_HEADS_PER_GROUP = 4
ATT_PATTERNS = ((128, 1), (512, 4), (2048, 16))
ATT_N_HEADS = ATT_HEADS_PER_GROUP * len(ATT_PATTERNS)
ATT_GROUP_DIM = ATT_HEADS_PER_GROUP * ATT_HEAD_DIM
ATT_BLOCK = 128
ATT_SPAN = ATT_BLOCK * max(r for _, r in ATT_PATTERNS)
ROPE_THETA = 10000.0

D_FF = 2816
FF_CHUNK = 256
EPS = 1e-6

LANES = 128
VMEM_LIMIT = 56 * 1024 * 1024

F32 = jnp.float32
BF16 = jnp.bfloat16


def _dot(a, b):
    return jnp.dot(a, b, preferred_element_type=F32)


def _dot_nt(a, b):
    return lax.dot_general(a, b, (((1,), (1,)), ((), ())), preferred_element_type=F32)


def _dot_tn(a, b):
    return lax.dot_general(a, b, (((0,), (0,)), ((), ())), preferred_element_type=F32)


def _sigmoid(x):
    return 1.0 / (1.0 + jnp.exp(-x))


def _rmsnorm_rows(x, g):
    return x * lax.rsqrt(jnp.mean(x * x, axis=-1, keepdims=True) + EPS) * g


def _const_spec(shape):
    zeros = (0,) * len(shape)
    return pl.BlockSpec(shape, lambda *_: zeros, pipeline_mode=pl.Buffered(1))


def _proj_cm_kernel(x_ref, g_ref, wzx_ref, wdt_ref, zt_ref, xbct_ref, dtt_ref):
    h = _rmsnorm_rows(x_ref[0], g_ref[...]).astype(BF16)
    rows = 1024
    for c in range(SSM_D_INNER // rows):
        zt_ref[0, c * rows:(c + 1) * rows, :] = _dot_nt(wzx_ref[c * rows:(c + 1) * rows, :], h).astype(BF16)
    for c in range(SSM_CONV_DIM // rows):
        w = wzx_ref[SSM_D_INNER + c * rows:SSM_D_INNER + (c + 1) * rows, :]
        xbct_ref[0, c * rows:(c + 1) * rows, :] = _dot_nt(w, h).astype(BF16)
    dtt_ref[0] = _dot_nt(wdt_ref[...], h)


def _proj_cm(x, g, wzx_t, wdt_t, tm):
    b, s, d = x.shape
    return pl.pallas_call(
        _proj_cm_kernel,
        grid=(b, s // tm),
        in_specs=[
            pl.BlockSpec((1, tm, d), lambda i, j: (i, j, 0)),
            _const_spec((1, d)),
            _const_spec(wzx_t.shape),
            _const_spec(wdt_t.shape),
        ],
        out_specs=[
            pl.BlockSpec((1, SSM_D_INNER, tm), lambda i, j: (i, 0, j)),
            pl.BlockSpec((1, SSM_CONV_DIM, tm), lambda i, j: (i, 0, j)),
            pl.BlockSpec((1, SSM_N_HEADS, tm), lambda i, j: (i, 0, j)),
        ],
        out_shape=[
            jax.ShapeDtypeStruct((b, SSM_D_INNER, s), BF16),
            jax.ShapeDtypeStruct((b, SSM_CONV_DIM, s), BF16),
            jax.ShapeDtypeStruct((b, SSM_N_HEADS, s), F32),
        ],
        compiler_params=pltpu.CompilerParams(
            dimension_semantics=("parallel", "parallel"), vmem_limit_bytes=VMEM_LIMIT),
        name="proj_cm",
    )(x, g, wzx_t, wdt_t)


def _rope(t, cos, sin):
    return t * cos + pltpu.roll(t, ATT_HEAD_DIM // 2, 1) * sin


def _proj_tm_kernel(x_ref, g_ref, w_ref, cq_ref, sq_ref, ck_ref, sk_ref, *out_refs):
    h = _rmsnorm_rows(x_ref[0], g_ref[...]).astype(BF16)
    n_groups = len(ATT_PATTERNS)
    for kind in range(3):
        for gi in range(n_groups):
            idx = kind * n_groups + gi
            acc = _dot(h, w_ref[:, idx * ATT_GROUP_DIM:(idx + 1) * ATT_GROUP_DIM])
            out = out_refs[idx]
            for j in range(ATT_HEADS_PER_GROUP):
                t = acc[:, j * ATT_HEAD_DIM:(j + 1) * ATT_HEAD_DIM]
                if kind == 0:
                    t = _rope(t, cq_ref[...], sq_ref[...])
                elif kind == 1:
                    t = _rope(t, ck_ref[...], sk_ref[...])
                out[0, :, j * ATT_HEAD_DIM:(j + 1) * ATT_HEAD_DIM] = t.astype(BF16)


def _proj_tm(x, g, w_qkv, rope_tabs, tm):
    b, s, d = x.shape
    tab_spec = pl.BlockSpec((tm, ATT_HEAD_DIM), lambda i, j: (j, 0))
    n_out = 3 * len(ATT_PATTERNS)
    return pl.pallas_call(
        _proj_tm_kernel,
        grid=(b, s // tm),
        in_specs=[
            pl.BlockSpec((1, tm, d), lambda i, j: (i, j, 0)),
            _const_spec((1, d)),
            _const_spec(w_qkv.shape),
            tab_spec, tab_spec, tab_spec, tab_spec,
        ],
        out_specs=[pl.BlockSpec((1, tm, ATT_GROUP_DIM), lambda i, j: (i, j, 0))] * n_out,
        out_shape=[jax.ShapeDtypeStruct((b, s, ATT_GROUP_DIM), BF16)] * n_out,
        compiler_params=pltpu.CompilerParams(
            dimension_semantics=("parallel", "parallel"), vmem_limit_bytes=VMEM_LIMIT),
        name="proj_tm",
    )(x, g, w_qkv, *rope_tabs)


def _ssd_kernel(zt_ref, xbct_ref, dtt_ref, cw_ref, cb_ref, dtb_ref, a_ref, dsk_ref, nw_ref, wout_ref,
                y_ref, prev_s, state_s, xbc_s, xdec_s, y_s):
    q = SSM_CHUNK

    @pl.when(pl.program_id(1) == 0)
    def _():
        prev_s[...] = jnp.zeros_like(prev_s)
        state_s[...] = jnp.zeros_like(state_s)

    rb = 256
    lane = lax.broadcasted_iota(jnp.int32, (rb, q), 1)
    for c in range(SSM_CONV_DIM // rb):
        sl = slice(c * rb, (c + 1) * rb)
        cur = xbct_ref[0, sl, :].astype(F32)
        prev = prev_s[sl, :].astype(F32)
        acc = cb_ref[sl, :] + cw_ref[SSM_CONV - 1, sl, :] * cur
        for s in range(1, SSM_CONV):
            shifted = pltpu.roll(jnp.where(lane >= q - s, prev, cur), s, 1)
            acc = acc + cw_ref[SSM_CONV - 1 - s, sl, :] * shifted
        xbc_s[sl, :] = acc * _sigmoid(acc)
    prev_s[...] = xbct_ref[0]

    dt_in = dtt_ref[0] + dtb_ref[...]
    dt = jnp.maximum(dt_in, 0.0) + jnp.log(1.0 + jnp.exp(-jnp.abs(dt_in)))
    a = dt * a_ref[...]
    si = lax.broadcasted_iota(jnp.int32, (q, q), 0)
    li = lax.broadcasted_iota(jnp.int32, (q, q), 1)
    upper = (si <= li).astype(BF16)
    a_hi = a.astype(BF16)
    r1 = a - a_hi.astype(F32)
    a_mid = r1.astype(BF16)
    a_lo = (r1 - a_mid.astype(F32)).astype(BF16)
    acs_t = _dot(a_hi, upper) + _dot(a_mid, upper) + _dot(a_lo, upper)
    acs = acs_t.T
    tot = jnp.broadcast_to(acs_t[:, q - 1:q], (SSM_N_HEADS, q))
    ecs_t = jnp.exp(acs_t)
    dec_t = jnp.exp(tot - acs_t)
    etot = jnp.exp(tot)
    causal = li >= si

    bc0 = SSM_D_INNER
    for g in range(SSM_N_GROUPS):
        b_t = xbc_s[bc0 + g * SSM_D_STATE:bc0 + (g + 1) * SSM_D_STATE, :]
        c_t = xbc_s[bc0 + SSM_BC_DIM + g * SSM_D_STATE:bc0 + SSM_BC_DIM + (g + 1) * SSM_D_STATE, :]
        b_g = b_t.T.astype(BF16)
        c_tb = c_t.astype(BF16)
        cb_t = _dot(b_g, c_tb)
        gsl = slice(g * SSM_GROUP_DIM, (g + 1) * SSM_GROUP_DIM)
        y_off = _dot(state_s[gsl, :].astype(BF16), c_tb)
        for j in range(SSM_HEADS_PER_GROUP):
            h = g * SSM_HEADS_PER_GROUP + j
            hsl = slice(h * SSM_HEAD_DIM, (h + 1) * SSM_HEAD_DIM)
            x_h = xbc_s[hsl, :]
            xdt = x_h * dt[h:h + 1, :]
            seg = jnp.where(causal, acs_t[h:h + 1, :] - acs[:, h:h + 1], -jnp.inf)
            m_t = (cb_t * jnp.exp(seg)).astype(BF16)
            y_diag = _dot(xdt.astype(BF16), m_t)
            xdec_s[hsl, :] = (xdt * dec_t[h:h + 1, :]).astype(BF16)
            y_s[hsl, :] = (y_diag + y_off[j * SSM_HEAD_DIM:(j + 1) * SSM_HEAD_DIM, :] * ecs_t[h:h + 1, :]
                           + dsk_ref[hsl, :] * x_h)
        new_states = _dot(xdec_s[gsl, :], b_g)
        for j in range(SSM_HEADS_PER_GROUP):
            h = g * SSM_HEADS_PER_GROUP + j
            hsl = slice(h * SSM_HEAD_DIM, (h + 1) * SSM_HEAD_DIM)
            state_s[hsl, :] = (state_s[hsl, :] * etot[h:h + 1, :]
                               + new_states[j * SSM_HEAD_DIM:(j + 1) * SSM_HEAD_DIM, :])

    for g in range(SSM_N_GROUPS):
        gsl = slice(g * SSM_GROUP_DIM, (g + 1) * SSM_GROUP_DIM)
        z = zt_ref[0, gsl, :].astype(F32)
        y = y_s[gsl, :] * (z * _sigmoid(z))
        ms = jnp.sum(y * y, axis=0, keepdims=True) * (1.0 / SSM_GROUP_DIM)
        xdec_s[gsl, :] = (y * lax.rsqrt(ms + EPS) * nw_ref[gsl, :]).astype(BF16)
    y_ref[0] = _dot_tn(xdec_s[...], wout_ref[...]).astype(BF16)


def _ssd(zt, xbct, dtt, cw, cb, dtb, a_neg, dsk, nw, wout):
    b, _, s = zt.shape
    q = SSM_CHUNK
    return pl.pallas_call(
        _ssd_kernel,
        grid=(b, s // q),
        in_specs=[
            pl.BlockSpec((1, SSM_D_INNER, q), lambda i, j: (i, 0, j)),
            pl.BlockSpec((1, SSM_CONV_DIM, q), lambda i, j: (i, 0, j)),
            pl.BlockSpec((1, SSM_N_HEADS, q), lambda i, j: (i, 0, j)),
            _const_spec(cw.shape), _const_spec(cb.shape), _const_spec(dtb.shape), _const_spec(a_neg.shape),
            _const_spec(dsk.shape), _const_spec(nw.shape), _const_spec(wout.shape),
        ],
        out_specs=pl.BlockSpec((1, q, D_MODEL), lambda i, j: (i, j, 0)),
        out_shape=jax.ShapeDtypeStruct((b, s, D_MODEL), BF16),
        scratch_shapes=[
            pltpu.VMEM((SSM_CONV_DIM, q), BF16),
            pltpu.VMEM((SSM_D_INNER, q), F32),
            pltpu.VMEM((SSM_CONV_DIM, q), F32),
            pltpu.VMEM((SSM_D_INNER, q), BF16),
            pltpu.VMEM((SSM_D_INNER, q), F32),
        ],
        compiler_params=pltpu.CompilerParams(
            dimension_semantics=("parallel", "arbitrary"), vmem_limit_bytes=VMEM_LIMIT),
        name="ssd",
    )(zt, xbct, dtt, cw, cb, dtb, a_neg, dsk, nw, wout)


def _attn_kernel(q_ref, kc_ref, kp_ref, vc_ref, vp_ref, o_ref, lse_ref, k_s, v_s, *, r, nbl):
    blk = ATT_BLOCK
    i = pl.program_id(1)
    k_s[0:blk, :] = kp_ref[0]
    k_s[blk:, :] = kc_ref[0]
    v_s[0:blk, :] = vp_ref[0]
    v_s[blk:, :] = vc_ref[0]

    qi = lax.broadcasted_iota(jnp.int32, (blk, 2 * blk), 0)
    kj = lax.broadcasted_iota(jnp.int32, (blk, 2 * blk), 1)
    dist = qi + blk - kj
    band = (dist >= 0) & (dist <= blk)
    slot_lanes = LANES // ATT_HEADS_PER_GROUP

    for rho in range(r):
        for j in range(ATT_HEADS_PER_GROUP):
            col = (rho * ATT_HEADS_PER_GROUP + j) * ATT_HEAD_DIM
            csl = slice(col, col + ATT_HEAD_DIM)
            lsl = slice(rho * LANES + j * slot_lanes, rho * LANES + (j + 1) * slot_lanes)

            def body(nl, carry, csl=csl, lsl=lsl):
                row = pl.multiple_of(nl * blk, blk)
                qb = q_ref[0, pl.ds(row, blk), csl]
                kb = k_s[pl.ds(row, 2 * blk), csl]
                vb = v_s[pl.ds(row, 2 * blk), csl]
                s = _dot_nt(qb, kb)
                first_key = jnp.where(i * nbl + nl > 0, 0, blk)
                s = jnp.where(band & (kj >= first_key), s, -jnp.inf)
                m = jnp.max(s, axis=-1, keepdims=True)
                p = jnp.exp(s - m)
                den = jnp.sum(p, axis=-1, keepdims=True)
                o = _dot(p.astype(BF16), vb) / den
                o_ref[0, pl.ds(row, blk), csl] = o.astype(BF16)
                lse_ref[0, pl.ds(row, blk), lsl] = jnp.broadcast_to(m + jnp.log(den), (blk, slot_lanes))
                return carry

            lax.fori_loop(0, nbl, body, 0)


def _attn_group(qg, kg, vg, r):
    b, s, gd = qg.shape
    nbl = ATT_SPAN // (ATT_BLOCK * r)
    rows = nbl * ATT_BLOCK
    cols = r * gd
    view = lambda t: t.reshape(b, s // r, cols)
    cur = pl.BlockSpec((1, rows, cols), lambda i, j: (i, j, 0))
    prev = pl.BlockSpec((1, ATT_BLOCK, cols), lambda i, j: (i, jnp.maximum(j * nbl - 1, 0), 0))
    o, lse = pl.pallas_call(
        functools.partial(_attn_kernel, r=r, nbl=nbl),
        grid=(b, s // ATT_SPAN),
        in_specs=[cur, cur, prev, cur, prev],
        out_specs=[cur, pl.BlockSpec((1, rows, r * LANES), lambda i, j: (i, j, 0))],
        out_shape=[
            jax.ShapeDtypeStruct((b, s // r, cols), BF16),
            jax.ShapeDtypeStruct((b, s // r, r * LANES), F32),
        ],
        scratch_shapes=[
            pltpu.VMEM((rows + ATT_BLOCK, cols), BF16),
            pltpu.VMEM((rows + ATT_BLOCK, cols), BF16),
        ],
        compiler_params=pltpu.CompilerParams(
            dimension_semantics=("parallel", "arbitrary"), vmem_limit_bytes=VMEM_LIMIT),
        name=f"attn_r{r}",
    )(view(qg), view(kg), view(kg), view(vg), view(vg))
    return o.reshape(b, s, gd), lse.reshape(b, s, LANES)


def _tail_kernel(x_ref, yssm_ref, o0_ref, o1_ref, o2_ref, l0_ref, l1_ref, l2_ref,
                 gmix_ref, wgate_ref, bgate_ref, watt_ref, wmix_ref, gffn_ref, wg_ref, wu_ref, wd_ref,
                 gfin_ref, out_ref):
    x = x_ref[0]
    h = _rmsnorm_rows(x, gmix_ref[...]).astype(BF16)
    gates = _sigmoid(_dot(h, wgate_ref[...]) + bgate_ref[...])

    lses = (l0_ref[0], l1_ref[0], l2_ref[0])
    outs = (o0_ref, o1_ref, o2_ref)
    m = jnp.maximum(jnp.maximum(lses[0], lses[1]), lses[2])
    es = [jnp.exp(l - m) for l in lses]
    inv = 1.0 / (es[0] + es[1] + es[2])
    slot_lanes = LANES // ATT_HEADS_PER_GROUP
    merged = []
    for j in range(ATT_HEADS_PER_GROUP):
        hs = slice(j * ATT_HEAD_DIM, (j + 1) * ATT_HEAD_DIM)
        acc = None
        for gi in range(len(ATT_PATTERNS)):
            wt = (es[gi] * inv)[:, j * slot_lanes:j * slot_lanes + 1]
            term = wt * outs[gi][0, :, hs].astype(F32)
            acc = term if acc is None else acc + term
        merged.append(acc.astype(BF16))
    att = jnp.concatenate(merged, axis=-1)
    y_att = _dot(att, watt_ref[...])

    mixed = gates[:, :D_MODEL] * yssm_ref[0].astype(F32) + gates[:, D_MODEL:] * y_att
    x1 = x + _dot(mixed.astype(BF16), wmix_ref[...])

    h2 = _rmsnorm_rows(x1, gffn_ref[...]).astype(BF16)
    acc = x1
    for c in range(D_FF // FF_CHUNK):
        csl = slice(c * FF_CHUNK, (c + 1) * FF_CHUNK)
        gate = _dot(h2, wg_ref[:, csl])
        up = _dot(h2, wu_ref[:, csl])
        act = (gate * _sigmoid(gate) * up).astype(BF16)
        acc = acc + _dot(act, wd_ref[csl, :])
    out_ref[0] = _rmsnorm_rows(acc, gfin_ref[...])


def _tail(x, yssm, outs, lses, gmix, wgate, bgate, watt, wmix, gffn, wg, wu, wd, gfin, t):
    b, s, d = x.shape
    tok = lambda w: pl.BlockSpec((1, t, w), lambda i, j: (i, j, 0))
    consts = (gmix, wgate, bgate, watt, wmix, gffn, wg, wu, wd, gfin)
    return pl.pallas_call(
        _tail_kernel,
        grid=(b, s // t),
        in_specs=[tok(d), tok(d)] + [tok(ATT_GROUP_DIM)] * 3 + [tok(LANES)] * 3
                 + [_const_spec(c.shape) for c in consts],
        out_specs=tok(d),
        out_shape=jax.ShapeDtypeStruct((b, s, d), F32),
        compiler_params=pltpu.CompilerParams(
            dimension_semantics=("parallel", "parallel"), vmem_limit_bytes=VMEM_LIMIT),
        name="tail",
    )(x, yssm, *outs, *lses, *consts)


def _rope_tables(s):
    half = ATT_HEAD_DIM // 2
    inv = ROPE_THETA ** (-jnp.arange(half, dtype=F32) / half)
    ang = jnp.arange(s).astype(F32)[:, None] * inv[None, :]
    cos = jnp.cos(ang)
    sin = jnp.sin(ang)
    cos_full = jnp.concatenate([cos, cos], axis=-1)
    sin_signed = jnp.concatenate([-sin, sin], axis=-1)
    scale = ATT_HEAD_DIM ** -0.5
    return cos_full * scale, sin_signed * scale, cos_full, sin_signed


def _layer(x, norm_mix, w_in, b_gate, conv_w, conv_b, dt_bias, a_log, d_skip, ssm_norm,
           w_ssm_out, w_att_out, w_mix_out, norm_ffn, w_ffn_gate, w_ffn_up, w_ffn_down, norm_out):
    b, s, d = x.shape
    assert d == D_MODEL and s % ATT_SPAN == 0
    q = SSM_CHUNK
    o_xbc = SSM_D_INNER
    o_dt = o_xbc + SSM_CONV_DIM
    o_qkv = o_dt + SSM_N_HEADS
    o_gate = o_qkv + 3 * ATT_N_HEADS * ATT_HEAD_DIM

    row = lambda v: v.astype(F32).reshape(1, -1)
    lanes = lambda v: jnp.broadcast_to(v.astype(F32)[..., None], v.shape + (q,))
    gmix = row(norm_mix)

    wzx_t = w_in[:, :o_dt].T.astype(BF16)
    wdt_t = w_in[:, o_dt:o_qkv].T.astype(BF16)
    zt, xbct, dtt = _proj_cm(x, gmix, wzx_t, wdt_t, tm=512)

    w_qkv = w_in[:, o_qkv:o_gate].astype(BF16)
    qkv = _proj_tm(x, gmix, w_qkv, _rope_tables(s), tm=512)

    yssm = _ssd(zt, xbct, dtt, lanes(conv_w), lanes(conv_b), dt_bias.astype(F32).reshape(-1, 1),
                (-jnp.exp(a_log.astype(F32))).reshape(-1, 1),
                lanes(jnp.repeat(d_skip, SSM_HEAD_DIM)), lanes(ssm_norm), w_ssm_out.astype(BF16))

    n_groups = len(ATT_PATTERNS)
    outs, lses = [], []
    for gi, (window, r) in enumerate(ATT_PATTERNS):
        assert window // r == ATT_BLOCK
        o, lse = _attn_group(qkv[gi], qkv[n_groups + gi], qkv[2 * n_groups + gi], r)
        outs.append(o)
        lses.append(lse)

    return _tail(x, yssm, outs, lses, gmix, w_in[:, o_gate:].astype(BF16), row(b_gate),
                 w_att_out.astype(BF16), w_mix_out.astype(BF16), row(norm_ffn),
                 w_ffn_gate.astype(BF16), w_ffn_up.astype(BF16), w_ffn_down.astype(BF16),
                 row(norm_out), t=256)


def kernel(x, norm_mix, w_in, b_gate, conv_w, conv_b, dt_bias, a_log, d_skip, ssm_norm, w_ssm_out,
           w_att_out, w_mix_out, norm_ffn, w_ffn_gate, w_ffn_up, w_ffn_down, norm_final):
    depth = w_in.shape[0]
    assert depth == 1, "the tail kernel fuses the final rmsnorm into the last (only) layer"
    return _layer(x, norm_mix[0], w_in[0], b_gate[0], conv_w[0], conv_b[0], dt_bias[0], a_log[0],
                  d_skip[0], ssm_norm[0], w_ssm_out[0], w_att_out[0], w_mix_out[0], norm_ffn[0],
                  w_ffn_gate[0], w_ffn_up[0], w_ffn_down[0], norm_final)
```

```python
import functools

import jax
import jax.numpy as jnp
from jax import lax
from jax.experimental import pallas as pl
from jax.experimental.pallas import tpu as pltpu

D_MODEL = 1024
SSM_D_INNER = 2048
SSM_HEAD_DIM = 64
SSM_N_HEADS = 32
SSM_N_GROUPS = 4
SSM_HEADS_PER_GROUP = SSM_N_HEADS // SSM_N_GROUPS
SSM_GROUP_DIM = SSM_D_INNER // SSM_N_GROUPS
SSM_D_STATE = 128
SSM_CONV = 4
SSM_CHUNK = 128
SSM_BC_DIM = SSM_N_GROUPS * SSM_D_STATE
SSM_CONV_DIM = SSM_D_INNER + 2 * SSM_BC_DIM

ATT_HEAD_DIM = 128
ATT_HEADS_PER_GROUP = 4
ATT_PATTERNS = ((128, 1), (512, 4), (2048, 16))
ATT_N_GROUPS = len(ATT_PATTERNS)
ATT_N_HEADS = ATT_HEADS_PER_GROUP * ATT_N_GROUPS
ATT_GROUP_DIM = ATT_HEADS_PER_GROUP * ATT_HEAD_DIM
ATT_BLOCK = 128
ATT_SPAN = ATT_BLOCK * max(r for _, r in ATT_PATTERNS)
ROPE_THETA = 10000.0

D_FF = 2816
FF_CHUNK = 256
EPS = 1e-6

LANES = 128
VMEM_LIMIT = 56 * 1024 * 1024

PROJ_TOKENS = 512
TAIL_TOKENS = 256

F32 = jnp.float32
BF16 = jnp.bfloat16


def _dot(a, b):
    return jnp.dot(a, b, preferred_element_type=F32)


def _dot_nt(a, b):
    return lax.dot_general(a, b, (((1,), (1,)), ((), ())), preferred_element_type=F32)


def _dot_tn(a, b):
    return lax.dot_general(a, b, (((0,), (0,)), ((), ())), preferred_element_type=F32)


def _sigmoid(x):
    return 1.0 / (1.0 + jnp.exp(-x))


def _rmsnorm_rows(x, g):
    return x * lax.rsqrt(jnp.mean(x * x, axis=-1, keepdims=True) + EPS) * g


def _const_spec(shape):
    zeros = (0,) * len(shape)
    return pl.BlockSpec(shape, lambda *_: zeros, pipeline_mode=pl.Buffered(1))


def _proj_cm_kernel(x_ref, g_ref, wzx_ref, wdt_ref, zt_ref, xbct_ref, dtt_ref):
    h = _rmsnorm_rows(x_ref[0], g_ref[...]).astype(BF16)
    rows = 1024
    for c in range(SSM_D_INNER // rows):
        zt_ref[0, c * rows:(c + 1) * rows, :] = _dot_nt(wzx_ref[c * rows:(c + 1) * rows, :], h).astype(BF16)
    for c in range(SSM_CONV_DIM // rows):
        w = wzx_ref[SSM_D_INNER + c * rows:SSM_D_INNER + (c + 1) * rows, :]
        xbct_ref[0, c * rows:(c + 1) * rows, :] = _dot_nt(w, h).astype(BF16)
    dtt_ref[0] = _dot_nt(wdt_ref[...], h)


def _proj_cm(x, g, wzx_t, wdt_t):
    b, s, d = x.shape
    tm = PROJ_TOKENS
    return pl.pallas_call(
        _proj_cm_kernel,
        grid=(b, s // tm),
        in_specs=[
            pl.BlockSpec((1, tm, d), lambda i, j: (i, j, 0)),
            _const_spec((1, d)),
            _const_spec(wzx_t.shape),
            _const_spec(wdt_t.shape),
        ],
        out_specs=[
            pl.BlockSpec((1, SSM_D_INNER, tm), lambda i, j: (i, 0, j)),
            pl.BlockSpec((1, SSM_CONV_DIM, tm), lambda i, j: (i, 0, j)),
            pl.BlockSpec((1, SSM_N_HEADS, tm), lambda i, j: (i, 0, j)),
        ],
        out_shape=[
            jax.ShapeDtypeStruct((b, SSM_D_INNER, s), BF16),
            jax.ShapeDtypeStruct((b, SSM_CONV_DIM, s), BF16),
            jax.ShapeDtypeStruct((b, SSM_N_HEADS, s), F32),
        ],
        compiler_params=pltpu.CompilerParams(
            dimension_semantics=("parallel", "parallel"), vmem_limit_bytes=VMEM_LIMIT),
        name="proj_cm",
    )(x, g, wzx_t, wdt_t)


def _proj_tm_kernel(x_ref, g_ref, w_ref, tab0_ref, tab1_ref, tab2_ref, *refs):
    out_refs, (h_s, hp_s) = refs[:3 * ATT_N_GROUPS], refs[3 * ATT_N_GROUPS:]
    tabs = (tab0_ref, tab1_ref, tab2_ref)
    tm = x_ref.shape[1]
    h = _rmsnorm_rows(x_ref[0], g_ref[...])
    n_slabs = D_MODEL // LANES
    for k in range(n_slabs):
        h_s[k] = h[:, k * LANES:(k + 1) * LANES]

    for gi, (_, r) in enumerate(ATT_PATTERNS):
        rows = tm // r
        if r == 1:
            hp = h.astype(BF16)
        else:
            for rho in range(r):
                for k in range(n_slabs):
                    hp_s[rho * rows:(rho + 1) * rows, k * LANES:(k + 1) * LANES] = (
                        h_s[k, pl.ds(rho, rows, stride=r), :].astype(BF16))
            hp = hp_s[...]
        acc = _dot(hp, w_ref[:, gi * 3 * ATT_GROUP_DIM:(gi + 1) * 3 * ATT_GROUP_DIM])
        tab = tabs[gi]
        for kind in range(3):
            out = out_refs[gi * 3 + kind]
            for j in range(ATT_HEADS_PER_GROUP):
                c0 = kind * ATT_GROUP_DIM + j * ATT_HEAD_DIM
                t = acc[:, c0:c0 + ATT_HEAD_DIM]
                if kind < 2:
                    t = t * tab[2 * kind] + pltpu.roll(t, ATT_HEAD_DIM // 2, 1) * tab[2 * kind + 1]
                t = t.astype(BF16)
                for rho in range(r):
                    o0 = (j * r + rho) * ATT_HEAD_DIM
                    out[0, :, o0:o0 + ATT_HEAD_DIM] = t[rho * rows:(rho + 1) * rows, :]


def _proj_tm(x, g, w_qkv, tabs):
    b, s, d = x.shape
    tm = PROJ_TOKENS
    tab_spec = pl.BlockSpec((4, tm, ATT_HEAD_DIM), lambda i, j: (0, j, 0))
    out_specs, out_shape = [], []
    for _, r in ATT_PATTERNS:
        for _ in range(3):
            out_specs.append(pl.BlockSpec((1, tm // r, r * ATT_GROUP_DIM), lambda i, j: (i, j, 0)))
            out_shape.append(jax.ShapeDtypeStruct((b, s // r, r * ATT_GROUP_DIM), BF16))
    return pl.pallas_call(
        _proj_tm_kernel,
        grid=(b, s // tm),
        in_specs=[
            pl.BlockSpec((1, tm, d), lambda i, j: (i, j, 0)),
            _const_spec((1, d)),
            _const_spec(w_qkv.shape),
            tab_spec, tab_spec, tab_spec,
        ],
        out_specs=out_specs,
        out_shape=out_shape,
        scratch_shapes=[
            pltpu.VMEM((d // LANES, tm, LANES), F32),
            pltpu.VMEM((tm, d), BF16),
        ],
        compiler_params=pltpu.CompilerParams(
            dimension_semantics=("parallel", "parallel"), vmem_limit_bytes=VMEM_LIMIT),
        name="proj_tm",
    )(x, g, w_qkv, *tabs)


def _ssd_kernel(zt_ref, xbct_ref, dtt_ref, cw_ref, cb_ref, dtb_ref, a_ref, dsk_ref, nw_ref, wout_ref,
                y_ref, prev_s, state_s, xbc_s, xdec_s, y_s):
    q = SSM_CHUNK

    @pl.when(pl.program_id(1) == 0)
    def _():
        prev_s[...] = jnp.zeros_like(prev_s)
        state_s[...] = jnp.zeros_like(state_s)

    rb = 256
    lane = lax.broadcasted_iota(jnp.int32, (rb, q), 1)
    for c in range(SSM_CONV_DIM // rb):
        sl = slice(c * rb, (c + 1) * rb)
        cur = xbct_ref[0, sl, :].astype(F32)
        prev = prev_s[sl, :].astype(F32)
        acc = cb_ref[sl, :] + cw_ref[SSM_CONV - 1, sl, :] * cur
        for s in range(1, SSM_CONV):
            shifted = pltpu.roll(jnp.where(lane >= q - s, prev, cur), s, 1)
            acc = acc + cw_ref[SSM_CONV - 1 - s, sl, :] * shifted
        xbc_s[sl, :] = acc * _sigmoid(acc)
    prev_s[...] = xbct_ref[0]

    dt_in = dtt_ref[0] + dtb_ref[...]
    dt = jnp.maximum(dt_in, 0.0) + jnp.log(1.0 + jnp.exp(-jnp.abs(dt_in)))
    a = dt * a_ref[...]
    si = lax.broadcasted_iota(jnp.int32, (q, q), 0)
    li = lax.broadcasted_iota(jnp.int32, (q, q), 1)
    upper = (si <= li).astype(BF16)
    a_hi = a.astype(BF16)
    r1 = a - a_hi.astype(F32)
    a_mid = r1.astype(BF16)
    a_lo = (r1 - a_mid.astype(F32)).astype(BF16)
    acs_t = _dot(a_hi, upper) + _dot(a_mid, upper) + _dot(a_lo, upper)
    acs = acs_t.T
    tot = jnp.broadcast_to(acs_t[:, q - 1:q], (SSM_N_HEADS, q))
    ecs_t = jnp.exp(acs_t)
    dec_t = jnp.exp(tot - acs_t)
    etot = jnp.exp(tot)
    causal = li >= si

    bc0 = SSM_D_INNER
    for g in range(SSM_N_GROUPS):
        b_t = xbc_s[bc0 + g * SSM_D_STATE:bc0 + (g + 1) * SSM_D_STATE, :]
        c_t = xbc_s[bc0 + SSM_BC_DIM + g * SSM_D_STATE:bc0 + SSM_BC_DIM + (g + 1) * SSM_D_STATE, :]
        b_g = b_t.T.astype(BF16)
        c_tb = c_t.astype(BF16)
        cb_t = _dot(b_g, c_tb)
        gsl = slice(g * SSM_GROUP_DIM, (g + 1) * SSM_GROUP_DIM)
        y_off = _dot(state_s[gsl, :].astype(BF16), c_tb)
        for j in range(SSM_HEADS_PER_GROUP):
            h = g * SSM_HEADS_PER_GROUP + j
            hsl = slice(h * SSM_HEAD_DIM, (h + 1) * SSM_HEAD_DIM)
            x_h = xbc_s[hsl, :]
            xdt = x_h * dt[h:h + 1, :]
            seg = jnp.where(causal, acs_t[h:h + 1, :] - acs[:, h:h + 1], -jnp.inf)
            m_t = (cb_t * jnp.exp(seg)).astype(BF16)
            y_diag = _dot(xdt.astype(BF16), m_t)
            xdec_s[hsl, :] = (xdt * dec_t[h:h + 1, :]).astype(BF16)
            y_s[hsl, :] = (y_diag + y_off[j * SSM_HEAD_DIM:(j + 1) * SSM_HEAD_DIM, :] * ecs_t[h:h + 1, :]
                           + dsk_ref[hsl, :] * x_h)
        new_states = _dot(xdec_s[gsl, :], b_g)
        for j in range(SSM_HEADS_PER_GROUP):
            h = g * SSM_HEADS_PER_GROUP + j
            hsl = slice(h * SSM_HEAD_DIM, (h + 1) * SSM_HEAD_DIM)
            state_s[hsl, :] = (state_s[hsl, :] * etot[h:h + 1, :]
                               + new_states[j * SSM_HEAD_DIM:(j + 1) * SSM_HEAD_DIM, :])

    for g in range(SSM_N_GROUPS):
        gsl = slice(g * SSM_GROUP_DIM, (g + 1) * SSM_GROUP_DIM)
        z = zt_ref[0, gsl, :].astype(F32)
        y = y_s[gsl, :] * (z * _sigmoid(z))
        ms = jnp.sum(y * y, axis=0, keepdims=True) * (1.0 / SSM_GROUP_DIM)
        xdec_s[gsl, :] = (y * lax.rsqrt(ms + EPS) * nw_ref[gsl, :]).astype(BF16)
    y_ref[0] = _dot_tn(xdec_s[...], wout_ref[...]).astype(BF16)


def _ssd(zt, xbct, dtt, cw, cb, dtb, a_neg, dsk, nw, wout):
    b, _, s = zt.shape
    q = SSM_CHUNK
    return pl.pallas_call(
        _ssd_kernel,
        grid=(b, s // q),
        in_specs=[
            pl.BlockSpec((1, SSM_D_INNER, q), lambda i, j: (i, 0, j)),
            pl.BlockSpec((1, SSM_CONV_DIM, q), lambda i, j: (i, 0, j)),
            pl.BlockSpec((1, SSM_N_HEADS, q), lambda i, j: (i, 0, j)),
            _const_spec(cw.shape), _const_spec(cb.shape), _const_spec(dtb.shape), _const_spec(a_neg.shape),
            _const_spec(dsk.shape), _const_spec(nw.shape), _const_spec(wout.shape),
        ],
        out_specs=pl.BlockSpec((1, q, D_MODEL), lambda i, j: (i, j, 0)),
        out_shape=jax.ShapeDtypeStruct((b, s, D_MODEL), BF16),
        scratch_shapes=[
            pltpu.VMEM((SSM_CONV_DIM, q), BF16),
            pltpu.VMEM((SSM_D_INNER, q), F32),
            pltpu.VMEM((SSM_CONV_DIM, q), F32),
            pltpu.VMEM((SSM_D_INNER, q), BF16),
            pltpu.VMEM((SSM_D_INNER, q), F32),
        ],
        compiler_params=pltpu.CompilerParams(
            dimension_semantics=("parallel", "arbitrary"), vmem_limit_bytes=VMEM_LIMIT),
        name="ssd",
    )(zt, xbct, dtt, cw, cb, dtb, a_neg, dsk, nw, wout)


def _attn_kernel(*refs):
    ins = refs[:5 * ATT_N_GROUPS]
    att_ref, o_s, l_s = refs[5 * ATT_N_GROUPS:]
    blk = ATT_BLOCK
    span = pl.program_id(1)

    qi = lax.broadcasted_iota(jnp.int32, (blk, 2 * blk), 0)
    kj = lax.broadcasted_iota(jnp.int32, (blk, 2 * blk), 1)
    dist = qi + blk - kj
    band = (dist >= 0) & (dist <= blk)
    band_first = band & (kj >= jnp.where(span > 0, 0, blk))

    for gi, (_, r) in enumerate(ATT_PATTERNS):
        q_ref, kc_ref, kp_ref, vc_ref, vp_ref = ins[5 * gi:5 * gi + 5]
        for nl in range(ATT_SPAN // (blk * r)):
            for rho in range(r):
                csl = slice(rho * ATT_HEAD_DIM, (rho + 1) * ATT_HEAD_DIM)
                qb = q_ref[0, nl * blk:(nl + 1) * blk, csl]
                if nl == 0:
                    kb = jnp.concatenate([kp_ref[0, :, csl], kc_ref[0, 0:blk, csl]], axis=0)
                    vb = jnp.concatenate([vp_ref[0, :, csl], vc_ref[0, 0:blk, csl]], axis=0)
                else:
                    kb = kc_ref[0, (nl - 1) * blk:(nl + 1) * blk, csl]
                    vb = vc_ref[0, (nl - 1) * blk:(nl + 1) * blk, csl]
                s = _dot_nt(qb, kb)
                s = jnp.where(band_first if nl == 0 else band, s, -jnp.inf)
                m = jnp.max(s, axis=-1, keepdims=True)
                p = jnp.exp(s - m)
                den = jnp.sum(p, axis=-1, keepdims=True)
                o = _dot(p.astype(BF16), vb) / den
                lse = jnp.broadcast_to(m + jnp.log(den), (blk, LANES))
                if r == 1:
                    rows = slice(nl * blk, (nl + 1) * blk)
                else:
                    rows = pl.ds(nl * blk * r + rho, blk, stride=r)
                o_s[gi, rows, :] = o
                l_s[gi, rows, :] = lse

    mrows = 256
    for c in range(ATT_SPAN // mrows):
        rsl = slice(c * mrows, (c + 1) * mrows)
        ls = [l_s[gi, rsl, :] for gi in range(ATT_N_GROUPS)]
        m = functools.reduce(jnp.maximum, ls)
        es = [jnp.exp(l - m) for l in ls]
        num = sum(e * o_s[gi, rsl, :] for gi, e in enumerate(es))
        att_ref[0, rsl, :] = (num / sum(es)).astype(BF16)


def _attn(qkv):
    b = qkv[0].shape[0]
    s = qkv[0].shape[1] * ATT_PATTERNS[0][1]
    in_specs, args = [], []
    for gi, (_, r) in enumerate(ATT_PATTERNS):
        nbl = ATT_SPAN // (ATT_BLOCK * r)
        cols = r * ATT_HEAD_DIM
        cur = pl.BlockSpec((1, nbl * ATT_BLOCK, cols), lambda i, sp, j: (i, sp, j))
        prev = pl.BlockSpec((1, ATT_BLOCK, cols),
                            lambda i, sp, j, nbl=nbl: (i, jnp.maximum(sp * nbl - 1, 0), j))
        q, k, v = qkv[3 * gi:3 * gi + 3]
        in_specs += [cur, cur, prev, cur, prev]
        args += [q, k, k, v, v]
    return pl.pallas_call(
        _attn_kernel,
        grid=(b, s // ATT_SPAN, ATT_HEADS_PER_GROUP),
        in_specs=in_specs,
        out_specs=pl.BlockSpec((1, ATT_SPAN, ATT_HEAD_DIM), lambda i, sp, j: (i, sp, j)),
        out_shape=jax.ShapeDtypeStruct((b, s, ATT_GROUP_DIM), BF16),
        scratch_shapes=[
            pltpu.VMEM((ATT_N_GROUPS, ATT_SPAN, LANES), F32),
            pltpu.VMEM((ATT_N_GROUPS, ATT_SPAN, LANES), F32),
        ],
        compiler_params=pltpu.CompilerParams(
            dimension_semantics=("parallel", "parallel", "parallel"), vmem_limit_bytes=VMEM_LIMIT),
        name="attn",
    )(*args)


def _tail_kernel(x_ref, yssm_ref, att_ref, gmix_ref, wgate_ref, bgate_ref, watt_ref, wmix_ref,
                 gffn_ref, wg_ref, wu_ref, wd_ref, gfin_ref, out_ref):
    x = x_ref[0]
    h = _rmsnorm_rows(x, gmix_ref[...]).astype(BF16)
    gates = _sigmoid(_dot(h, wgate_ref[...]) + bgate_ref[...])
    y_att = _dot(att_ref[0], watt_ref[...])
    mixed = gates[:, :D_MODEL] * yssm_ref[0].astype(F32) + gates[:, D_MODEL:] * y_att
    x1 = x + _dot(mixed.astype(BF16), wmix_ref[...])

    h2 = _rmsnorm_rows(x1, gffn_ref[...]).astype(BF16)
    acc = x1
    for c in range(D_FF // FF_CHUNK):
        csl = slice(c * FF_CHUNK, (c + 1) * FF_CHUNK)
        gate = _dot(h2, wg_ref[:, csl])
        up = _dot(h2, wu_ref[:, csl])
        act = (gate * _sigmoid(gate) * up).astype(BF16)
        acc = acc + _dot(act, wd_ref[csl, :])
    out_ref[0] = _rmsnorm_rows(acc, gfin_ref[...])


def _tail(x, yssm, att, *consts):
    b, s, d = x.shape
    t = TAIL_TOKENS
    tok = lambda w: pl.BlockSpec((1, t, w), lambda i, j: (i, j, 0))
    return pl.pallas_call(
        _tail_kernel,
        grid=(b, s // t),
        in_specs=[tok(d), tok(d), tok(ATT_GROUP_DIM)] + [_const_spec(c.shape) for c in consts],
        out_specs=tok(d),
        out_shape=jax.ShapeDtypeStruct((b, s, d), F32),
        compiler_params=pltpu.CompilerParams(
            dimension_semantics=("parallel", "parallel"), vmem_limit_bytes=VMEM_LIMIT),
        name="tail",
    )(x, yssm, att, *consts)


def _rope_tables(s, r, tm):
    half = ATT_HEAD_DIM // 2
    inv = ROPE_THETA ** (-jnp.arange(half, dtype=F32) / half)
    pos = jnp.arange(s).reshape(s // tm, tm // r, r).transpose(0, 2, 1).reshape(s)
    ang = pos.astype(F32)[:, None] * inv[None, :]
    cos = jnp.cos(ang)
    sin = jnp.sin(ang)
    cos_full = jnp.concatenate([cos, cos], axis=-1)
    sin_signed = jnp.concatenate([-sin, sin], axis=-1)
    scale = ATT_HEAD_DIM ** -0.5
    return jnp.stack([cos_full * scale, sin_signed * scale, cos_full, sin_signed])


def _layer(x, norm_mix, w_in, b_gate, conv_w, conv_b, dt_bias, a_log, d_skip, ssm_norm,
           w_ssm_out, w_att_out, w_mix_out, norm_ffn, w_ffn_gate, w_ffn_up, w_ffn_down, norm_out):
    b, s, d = x.shape
    assert d == D_MODEL and s % ATT_SPAN == 0
    q = SSM_CHUNK
    o_xbc = SSM_D_INNER
    o_dt = o_xbc + SSM_CONV_DIM
    o_qkv = o_dt + SSM_N_HEADS
    o_gate = o_qkv + 3 * ATT_N_HEADS * ATT_HEAD_DIM

    row = lambda v: v.astype(F32).reshape(1, -1)
    lanes = lambda v: jnp.broadcast_to(v.astype(F32)[..., None], v.shape + (q,))
    gmix = row(norm_mix)

    wzx_t = w_in[:, :o_dt].T.astype(BF16)
    wdt_t = w_in[:, o_dt:o_qkv].T.astype(BF16)
    zt, xbct, dtt = _proj_cm(x, gmix, wzx_t, wdt_t)

    w_qkv = w_in[:, o_qkv:o_gate].reshape(d, 3, ATT_N_GROUPS, ATT_GROUP_DIM)
    w_qkv = w_qkv.transpose(0, 2, 1, 3).reshape(d, 3 * ATT_N_GROUPS * ATT_GROUP_DIM).astype(BF16)
    tabs = [_rope_tables(s, r, PROJ_TOKENS) for _, r in ATT_PATTERNS]
    for window, r in ATT_PATTERNS:
        assert window // r == ATT_BLOCK
    qkv = _proj_tm(x, gmix, w_qkv, tabs)

    yssm = _ssd(zt, xbct, dtt, lanes(conv_w), lanes(conv_b), dt_bias.astype(F32).reshape(-1, 1),
                (-jnp.exp(a_log.astype(F32))).reshape(-1, 1),
                lanes(jnp.repeat(d_skip, SSM_HEAD_DIM)), lanes(ssm_norm), w_ssm_out.astype(BF16))

    att = _attn(qkv)

    return _tail(x, yssm, att, gmix, w_in[:, o_gate:].astype(BF16), row(b_gate),
                 w_att_out.astype(BF16), w_mix_out.astype(BF16), row(norm_ffn),
                 w_ffn_gate.astype(BF16), w_ffn_up.astype(BF16), w_ffn_down.astype(BF16), row(norm_out))


def kernel(x, norm_mix, w_in, b_gate, conv_w, conv_b, dt_bias, a_log, d_skip, ssm_norm, w_ssm_out,
           w_att_out, w_mix_out, norm_ffn, w_ffn_gate, w_ffn_up, w_ffn_down, norm_final):
    depth = w_in.shape[0]
    assert depth == 1, "the tail kernel fuses the final rmsnorm into the last (only) layer"
    return _layer(x, norm_mix[0], w_in[0], b_gate[0], conv_w[0], conv_b[0], dt_bias[0], a_log[0],
                  d_skip[0], ssm_norm[0], w_ssm_out[0], w_att_out[0], w_mix_out[0], norm_ffn[0],
                  w_ffn_gate[0], w_ffn_up[0], w_ffn_down[0], norm_final)
```

```python
import functools

import jax
import jax.numpy as jnp
from jax import lax
from jax.experimental import pallas as pl
from jax.experimental.pallas import tpu as pltpu

D_MODEL = 1024
SSM_D_INNER = 2048
SSM_HEAD_DIM = 64
SSM_N_HEADS = 32
SSM_N_GROUPS = 4
SSM_HEADS_PER_GROUP = SSM_N_HEADS // SSM_N_GROUPS
SSM_GROUP_DIM = SSM_D_INNER // SSM_N_GROUPS
SSM_D_STATE = 128
SSM_CONV = 4
SSM_CHUNK = 128
SSM_BC_DIM = SSM_N_GROUPS * SSM_D_STATE
SSM_CONV_DIM = SSM_D_INNER + 2 * SSM_BC_DIM

ATT_HEAD_DIM = 128
ATT_HEADS_PER_GROUP = 4
ATT_PATTERNS = ((128, 1), (512, 4), (2048, 16))
ATT_N_GROUPS = len(ATT_PATTERNS)
ATT_N_HEADS = ATT_HEADS_PER_GROUP * ATT_N_GROUPS
ATT_GROUP_DIM = ATT_HEADS_PER_GROUP * ATT_HEAD_DIM
ATT_BLOCK = 128
ATT_SPAN = ATT_BLOCK * max(r for _, r in ATT_PATTERNS)
ROPE_THETA = 10000.0

D_FF = 2816
FF_CHUNK = 256
EPS = 1e-6

LANES = 128
VMEM_LIMIT = 56 * 1024 * 1024

PROJ_TOKENS = 512
SSM_TILE_CHUNKS = 4
PROJ_ROWS = 256
TAIL_TOKENS = 512
TAIL_PARTS = 1

F32 = jnp.float32
BF16 = jnp.bfloat16


def _dot(a, b):
    return jnp.dot(a, b, preferred_element_type=F32)


def _dot_nt(a, b):
    return lax.dot_general(a, b, (((1,), (1,)), ((), ())), preferred_element_type=F32)


def _dot_tn(a, b):
    return lax.dot_general(a, b, (((0,), (0,)), ((), ())), preferred_element_type=F32)


def _sigmoid(x):
    return 1.0 / (1.0 + jnp.exp(-x))


def _rmsnorm_rows(x, g):
    return x * lax.rsqrt(jnp.mean(x * x, axis=-1, keepdims=True) + EPS) * g


def _const_spec(shape):
    zeros = (0,) * len(shape)
    return pl.BlockSpec(shape, lambda *_: zeros, pipeline_mode=pl.Buffered(1))


def _proj_tm_kernel(x_ref, g_ref, w_ref, tab0_ref, tab1_ref, tab2_ref, *refs):
    out_refs, (h_s, hp_s) = refs[:3 * ATT_N_GROUPS], refs[3 * ATT_N_GROUPS:]
    tabs = (tab0_ref, tab1_ref, tab2_ref)
    tm = x_ref.shape[1]
    h = _rmsnorm_rows(x_ref[0], g_ref[...])
    n_slabs = D_MODEL // LANES
    for k in range(n_slabs):
        h_s[k] = h[:, k * LANES:(k + 1) * LANES]

    for gi, (_, r) in enumerate(ATT_PATTERNS):
        rows = tm // r
        if r == 1:
            hp = h.astype(BF16)
        else:
            for rho in range(r):
                for k in range(n_slabs):
                    hp_s[rho * rows:(rho + 1) * rows, k * LANES:(k + 1) * LANES] = (
                        h_s[k, pl.ds(rho, rows, stride=r), :].astype(BF16))
            hp = hp_s[...]
        acc = _dot(hp, w_ref[:, gi * 3 * ATT_GROUP_DIM:(gi + 1) * 3 * ATT_GROUP_DIM])
        tab = tabs[gi]
        for kind in range(3):
            out = out_refs[gi * 3 + kind]
            for j in range(ATT_HEADS_PER_GROUP):
                c0 = kind * ATT_GROUP_DIM + j * ATT_HEAD_DIM
                t = acc[:, c0:c0 + ATT_HEAD_DIM]
                if kind < 2:
                    t = t * tab[2 * kind] + pltpu.roll(t, ATT_HEAD_DIM // 2, 1) * tab[2 * kind + 1]
                t = t.astype(BF16)
                for rho in range(r):
                    o0 = (j * r + rho) * ATT_HEAD_DIM
                    out[0, :, o0:o0 + ATT_HEAD_DIM] = t[rho * rows:(rho + 1) * rows, :]


def _proj_tm(x, g, w_qkv, tabs):
    b, s, d = x.shape
    tm = PROJ_TOKENS
    tab_spec = pl.BlockSpec((4, tm, ATT_HEAD_DIM), lambda i, j: (0, j, 0))
    out_specs, out_shape = [], []
    for _, r in ATT_PATTERNS:
        for _ in range(3):
            out_specs.append(pl.BlockSpec((1, tm // r, r * ATT_GROUP_DIM), lambda i, j: (i, j, 0)))
            out_shape.append(jax.ShapeDtypeStruct((b, s // r, r * ATT_GROUP_DIM), BF16))
    return pl.pallas_call(
        _proj_tm_kernel,
        grid=(b, s // tm),
        in_specs=[
            pl.BlockSpec((1, tm, d), lambda i, j: (i, j, 0)),
            _const_spec((1, d)),
            _const_spec(w_qkv.shape),
            tab_spec, tab_spec, tab_spec,
        ],
        out_specs=out_specs,
        out_shape=out_shape,
        scratch_shapes=[
            pltpu.VMEM((d // LANES, tm, LANES), F32),
            pltpu.VMEM((tm, d), BF16),
        ],
        compiler_params=pltpu.CompilerParams(
            dimension_semantics=("parallel", "parallel"), vmem_limit_bytes=VMEM_LIMIT),
        name="proj_tm",
    )(x, g, w_qkv, *tabs)


def _ssm_kernel(x_ref, g_ref, wz_ref, wx_ref, wdt_ref, cw_ref, cb_ref, dtb_ref, a_ref, dsk_ref, nw_ref,
                wout_ref, y_ref, h_s, z_s, x_s, dt_s, zc_s, prev_s, state_s, xbc_s, xdec_s, y_s, *,
                tiles_per_seq):
    g = pl.program_id(0)
    nxt = lax.rem(g, 2)
    cur = 1 - nxt

    @pl.when(g == 0)
    def _():
        z_s[1] = jnp.zeros(z_s.shape[1:], z_s.dtype)
        x_s[1] = jnp.zeros(x_s.shape[1:], x_s.dtype)
        dt_s[1] = jnp.zeros(dt_s.shape[1:], dt_s.dtype)

    @pl.when(lax.rem(jnp.maximum(g - 1, 0), tiles_per_seq) == 0)
    def _():
        prev_s[...] = jnp.zeros_like(prev_s)
        state_s[...] = jnp.zeros_like(state_s)

    h_s[...] = _rmsnorm_rows(x_ref[0], g_ref[...]).astype(BF16)
    dt_all = _dot_nt(wdt_ref[...], h_s[...])
    for k in range(SSM_TILE_CHUNKS):
        dt_s[nxt, k] = dt_all[:, k * SSM_CHUNK:(k + 1) * SSM_CHUNK]

    def chunk(c, carry):
        _conv_chunk(x_s.at[cur, c], cw_ref, cb_ref, prev_s, xbc_s)
        zc_s[...] = z_s[cur, c]
        dt_raw = dt_s[cur, c]

        for w_ref, buf, rows in ((wz_ref, z_s, SSM_D_INNER // SSM_TILE_CHUNKS),
                                 (wx_ref, x_s, SSM_CONV_DIM // SSM_TILE_CHUNKS)):
            for r in range(0, rows, PROJ_ROWS):
                r0 = pl.multiple_of(c * rows + r, PROJ_ROWS)
                part = _dot_nt(w_ref[pl.ds(r0, PROJ_ROWS), :], h_s[...])
                for k in range(SSM_TILE_CHUNKS):
                    buf[nxt, k, pl.ds(r0, PROJ_ROWS), :] = (
                        part[:, k * SSM_CHUNK:(k + 1) * SSM_CHUNK].astype(BF16))

        y_ref[0, pl.ds(pl.multiple_of(c * SSM_CHUNK, SSM_CHUNK), SSM_CHUNK), :] = _ssd_chunk(
            zc_s, dt_raw, dtb_ref, a_ref, dsk_ref, nw_ref, wout_ref, state_s, xbc_s, xdec_s, y_s)
        return carry

    lax.fori_loop(0, SSM_TILE_CHUNKS, chunk, 0)


def _conv_chunk(xbct_ref, cw_ref, cb_ref, prev_s, xbc_s):
    q = SSM_CHUNK
    rb = 256
    lane = lax.broadcasted_iota(jnp.int32, (rb, q), 1)
    for c in range(SSM_CONV_DIM // rb):
        sl = slice(c * rb, (c + 1) * rb)
        cur = xbct_ref[sl, :].astype(F32)
        prev = prev_s[sl, :].astype(F32)
        acc = cb_ref[sl, :] + cw_ref[SSM_CONV - 1, sl, :] * cur
        for s in range(1, SSM_CONV):
            shifted = pltpu.roll(jnp.where(lane >= q - s, prev, cur), s, 1)
            acc = acc + cw_ref[SSM_CONV - 1 - s, sl, :] * shifted
        xbc_s[sl, :] = acc * _sigmoid(acc)
    prev_s[...] = xbct_ref[...]


def _ssd_chunk(zt_ref, dt_raw, dtb_ref, a_ref, dsk_ref, nw_ref, wout_ref, state_s, xbc_s, xdec_s, y_s):
    q = SSM_CHUNK

    dt_in = dt_raw + dtb_ref[...]
    dt = jnp.maximum(dt_in, 0.0) + jnp.log(1.0 + jnp.exp(-jnp.abs(dt_in)))
    a = dt * a_ref[...]
    si = lax.broadcasted_iota(jnp.int32, (q, q), 0)
    li = lax.broadcasted_iota(jnp.int32, (q, q), 1)
    upper = (si <= li).astype(BF16)
    a_hi = a.astype(BF16)
    r1 = a - a_hi.astype(F32)
    a_mid = r1.astype(BF16)
    a_lo = (r1 - a_mid.astype(F32)).astype(BF16)
    acs_t = _dot(a_hi, upper) + _dot(a_mid, upper) + _dot(a_lo, upper)
    acs = acs_t.T
    tot = jnp.broadcast_to(acs_t[:, q - 1:q], (SSM_N_HEADS, q))
    ecs_t = jnp.exp(acs_t)
    dec_t = jnp.exp(tot - acs_t)
    etot = jnp.exp(tot)
    causal = li >= si

    bc0 = SSM_D_INNER
    for g in range(SSM_N_GROUPS):
        b_t = xbc_s[bc0 + g * SSM_D_STATE:bc0 + (g + 1) * SSM_D_STATE, :]
        c_t = xbc_s[bc0 + SSM_BC_DIM + g * SSM_D_STATE:bc0 + SSM_BC_DIM + (g + 1) * SSM_D_STATE, :]
        b_g = b_t.T.astype(BF16)
        c_tb = c_t.astype(BF16)
        cb_t = _dot(b_g, c_tb)
        gsl = slice(g * SSM_GROUP_DIM, (g + 1) * SSM_GROUP_DIM)
        y_off = _dot(state_s[gsl, :].astype(BF16), c_tb)
        for j in range(SSM_HEADS_PER_GROUP):
            h = g * SSM_HEADS_PER_GROUP + j
            hsl = slice(h * SSM_HEAD_DIM, (h + 1) * SSM_HEAD_DIM)
            x_h = xbc_s[hsl, :]
            xdt = x_h * dt[h:h + 1, :]
            seg = jnp.where(causal, acs_t[h:h + 1, :] - acs[:, h:h + 1], -jnp.inf)
            m_t = (cb_t * jnp.exp(seg)).astype(BF16)
            y_diag = _dot(xdt.astype(BF16), m_t)
            xdec_s[hsl, :] = (xdt * dec_t[h:h + 1, :]).astype(BF16)
            y_s[hsl, :] = (y_diag + y_off[j * SSM_HEAD_DIM:(j + 1) * SSM_HEAD_DIM, :] * ecs_t[h:h + 1, :]
                           + dsk_ref[hsl, :] * x_h)
        new_states = _dot(xdec_s[gsl, :], b_g)
        for j in range(SSM_HEADS_PER_GROUP):
            h = g * SSM_HEADS_PER_GROUP + j
            hsl = slice(h * SSM_HEAD_DIM, (h + 1) * SSM_HEAD_DIM)
            state_s[hsl, :] = (state_s[hsl, :] * etot[h:h + 1, :]
                               + new_states[j * SSM_HEAD_DIM:(j + 1) * SSM_HEAD_DIM, :])

    for g in range(SSM_N_GROUPS):
        gsl = slice(g * SSM_GROUP_DIM, (g + 1) * SSM_GROUP_DIM)
        z = zt_ref[gsl, :].astype(F32)
        y = y_s[gsl, :] * (z * _sigmoid(z))
        ms = jnp.sum(y * y, axis=0, keepdims=True) * (1.0 / SSM_GROUP_DIM)
        xdec_s[gsl, :] = (y * lax.rsqrt(ms + EPS) * nw_ref[gsl, :]).astype(BF16)
    return _dot_tn(xdec_s[...], wout_ref[...]).astype(BF16)


def _ssm(x, g, wz_t, wx_t, wdt_t, cw, cb, dtb, a_neg, dsk, nw, wout):
    b, s, d = x.shape
    q = SSM_CHUNK
    tile = SSM_TILE_CHUNKS * q
    tiles_per_seq = s // tile
    n_tiles = b * tiles_per_seq

    def proj_tile(i):
        t = jnp.minimum(i, n_tiles - 1)
        return (t // tiles_per_seq, t % tiles_per_seq, 0)

    def scan_tile(i):
        t = jnp.maximum(i - 1, 0)
        return (t // tiles_per_seq, t % tiles_per_seq, 0)

    consts = (g, wz_t, wx_t, wdt_t, cw, cb, dtb, a_neg, dsk, nw, wout)
    return pl.pallas_call(
        functools.partial(_ssm_kernel, tiles_per_seq=tiles_per_seq),
        grid=(n_tiles + 1,),
        in_specs=[pl.BlockSpec((1, tile, d), proj_tile)] + [_const_spec(c.shape) for c in consts],
        out_specs=pl.BlockSpec((1, tile, D_MODEL), scan_tile),
        out_shape=jax.ShapeDtypeStruct((b, s, D_MODEL), BF16),
        scratch_shapes=[
            pltpu.VMEM((tile, d), BF16),
            pltpu.VMEM((2, SSM_TILE_CHUNKS, SSM_D_INNER, q), BF16),
            pltpu.VMEM((2, SSM_TILE_CHUNKS, SSM_CONV_DIM, q), BF16),
            pltpu.VMEM((2, SSM_TILE_CHUNKS, SSM_N_HEADS, q), F32),
            pltpu.VMEM((SSM_D_INNER, q), BF16),
            pltpu.VMEM((SSM_CONV_DIM, q), BF16),
            pltpu.VMEM((SSM_D_INNER, q), F32),
            pltpu.VMEM((SSM_CONV_DIM, q), F32),
            pltpu.VMEM((SSM_D_INNER, q), BF16),
            pltpu.VMEM((SSM_D_INNER, q), F32),
        ],
        compiler_params=pltpu.CompilerParams(
            dimension_semantics=("arbitrary",), vmem_limit_bytes=VMEM_LIMIT),
        name="ssm",
    )(x, *consts)


def _attn_kernel(*refs):
    ins = refs[:5 * ATT_N_GROUPS]
    att_ref, o_s, l_s = refs[5 * ATT_N_GROUPS:]
    blk = ATT_BLOCK
    span = pl.program_id(1)

    qi = lax.broadcasted_iota(jnp.int32, (blk, 2 * blk), 0)
    kj = lax.broadcasted_iota(jnp.int32, (blk, 2 * blk), 1)
    dist = qi + blk - kj
    band = (dist >= 0) & (dist <= blk)
    band_first = band & (kj >= jnp.where(span > 0, 0, blk))

    for gi, (_, r) in enumerate(ATT_PATTERNS):
        q_ref, kc_ref, kp_ref, vc_ref, vp_ref = ins[5 * gi:5 * gi + 5]
        for nl in range(ATT_SPAN // (blk * r)):
            for rho in range(r):
                csl = slice(rho * ATT_HEAD_DIM, (rho + 1) * ATT_HEAD_DIM)
                qb = q_ref[0, nl * blk:(nl + 1) * blk, csl]
                if nl == 0:
                    kb = jnp.concatenate([kp_ref[0, :, csl], kc_ref[0, 0:blk, csl]], axis=0)
                    vb = jnp.concatenate([vp_ref[0, :, csl], vc_ref[0, 0:blk, csl]], axis=0)
                else:
                    kb = kc_ref[0, (nl - 1) * blk:(nl + 1) * blk, csl]
                    vb = vc_ref[0, (nl - 1) * blk:(nl + 1) * blk, csl]
                s = _dot_nt(qb, kb)
                s = jnp.where(band_first if nl == 0 else band, s, -jnp.inf)
                m = jnp.max(s, axis=-1, keepdims=True)
                p = jnp.exp(s - m)
                den = jnp.sum(p, axis=-1, keepdims=True)
                o = _dot(p.astype(BF16), vb) / den
                lse = jnp.broadcast_to(m + jnp.log(den), (blk, LANES))
                if r == 1:
                    rows = slice(nl * blk, (nl + 1) * blk)
                else:
                    rows = pl.ds(nl * blk * r + rho, blk, stride=r)
                o_s[gi, rows, :] = o
                l_s[gi, rows, :] = lse

    mrows = 256
    for c in range(ATT_SPAN // mrows):
        rsl = slice(c * mrows, (c + 1) * mrows)
        ls = [l_s[gi, rsl, :] for gi in range(ATT_N_GROUPS)]
        m = functools.reduce(jnp.maximum, ls)
        es = [jnp.exp(l - m) for l in ls]
        num = sum(e * o_s[gi, rsl, :] for gi, e in enumerate(es))
        att_ref[0, rsl, :] = (num / sum(es)).astype(BF16)


def _attn(qkv):
    b = qkv[0].shape[0]
    s = qkv[0].shape[1] * ATT_PATTERNS[0][1]
    in_specs, args = [], []
    for gi, (_, r) in enumerate(ATT_PATTERNS):
        nbl = ATT_SPAN // (ATT_BLOCK * r)
        cols = r * ATT_HEAD_DIM
        cur = pl.BlockSpec((1, nbl * ATT_BLOCK, cols), lambda i, sp, j: (i, sp, j))
        prev = pl.BlockSpec((1, ATT_BLOCK, cols),
                            lambda i, sp, j, nbl=nbl: (i, jnp.maximum(sp * nbl - 1, 0), j))
        q, k, v = qkv[3 * gi:3 * gi + 3]
        in_specs += [cur, cur, prev, cur, prev]
        args += [q, k, k, v, v]
    return pl.pallas_call(
        _attn_kernel,
        grid=(b, s // ATT_SPAN, ATT_HEADS_PER_GROUP),
        in_specs=in_specs,
        out_specs=pl.BlockSpec((1, ATT_SPAN, ATT_HEAD_DIM), lambda i, sp, j: (i, sp, j)),
        out_shape=jax.ShapeDtypeStruct((b, s, ATT_GROUP_DIM), BF16),
        scratch_shapes=[
            pltpu.VMEM((ATT_N_GROUPS, ATT_SPAN, LANES), F32),
            pltpu.VMEM((ATT_N_GROUPS, ATT_SPAN, LANES), F32),
        ],
        compiler_params=pltpu.CompilerParams(
            dimension_semantics=("parallel", "parallel", "parallel"), vmem_limit_bytes=VMEM_LIMIT),
        name="attn",
    )(*args)


def _tail_kernel(x_ref, yssm_ref, att_ref, gmix_ref, wgate_ref, bgate_ref, watt_ref, wmix_ref,
                 gffn_ref, wg_ref, wu_ref, wd_ref, gfin_ref, out_ref):
    rows = x_ref.shape[1] // TAIL_PARTS
    for part in range(TAIL_PARTS):
        rsl = slice(part * rows, (part + 1) * rows)
        x = x_ref[0, rsl, :]
        h = _rmsnorm_rows(x, gmix_ref[...]).astype(BF16)
        gates = _sigmoid(_dot(h, wgate_ref[...]) + bgate_ref[...])
        y_att = _dot(att_ref[0, rsl, :], watt_ref[...])
        mixed = gates[:, :D_MODEL] * yssm_ref[0, rsl, :].astype(F32) + gates[:, D_MODEL:] * y_att
        x1 = x + _dot(mixed.astype(BF16), wmix_ref[...])

        h2 = _rmsnorm_rows(x1, gffn_ref[...]).astype(BF16)
        acc = x1
        for c in range(D_FF // FF_CHUNK):
            csl = slice(c * FF_CHUNK, (c + 1) * FF_CHUNK)
            gate = _dot(h2, wg_ref[:, csl])
            up = _dot(h2, wu_ref[:, csl])
            act = (gate * _sigmoid(gate) * up).astype(BF16)
            acc = acc + _dot(act, wd_ref[csl, :])
        out_ref[0, rsl, :] = _rmsnorm_rows(acc, gfin_ref[...])


def _tail(x, yssm, att, *consts):
    b, s, d = x.shape
    t = TAIL_TOKENS
    tok = lambda w: pl.BlockSpec((1, t, w), lambda i, j: (i, j, 0))
    return pl.pallas_call(
        _tail_kernel,
        grid=(b, s // t),
        in_specs=[tok(d), tok(d), tok(ATT_GROUP_DIM)] + [_const_spec(c.shape) for c in consts],
        out_specs=tok(d),
        out_shape=jax.ShapeDtypeStruct((b, s, d), F32),
        compiler_params=pltpu.CompilerParams(
            dimension_semantics=("parallel", "parallel"), vmem_limit_bytes=VMEM_LIMIT),
        name="tail",
    )(x, yssm, att, *consts)


def _rope_tables(s, r, tm):
    half = ATT_HEAD_DIM // 2
    inv = ROPE_THETA ** (-jnp.arange(half, dtype=F32) / half)
    pos = jnp.arange(s).reshape(s // tm, tm // r, r).transpose(0, 2, 1).reshape(s)
    ang = pos.astype(F32)[:, None] * inv[None, :]
    cos = jnp.cos(ang)
    sin = jnp.sin(ang)
    cos_full = jnp.concatenate([cos, cos], axis=-1)
    sin_signed = jnp.concatenate([-sin, sin], axis=-1)
    scale = ATT_HEAD_DIM ** -0.5
    return jnp.stack([cos_full * scale, sin_signed * scale, cos_full, sin_signed])


def _layer(x, norm_mix, w_in, b_gate, conv_w, conv_b, dt_bias, a_log, d_skip, ssm_norm,
           w_ssm_out, w_att_out, w_mix_out, norm_ffn, w_ffn_gate, w_ffn_up, w_ffn_down, norm_out):
    b, s, d = x.shape
    assert d == D_MODEL and s % ATT_SPAN == 0
    q = SSM_CHUNK
    o_xbc = SSM_D_INNER
    o_dt = o_xbc + SSM_CONV_DIM
    o_qkv = o_dt + SSM_N_HEADS
    o_gate = o_qkv + 3 * ATT_N_HEADS * ATT_HEAD_DIM

    row = lambda v: v.astype(F32).reshape(1, -1)
    lanes = lambda v: jnp.broadcast_to(v.astype(F32)[..., None], v.shape + (q,))
    gmix = row(norm_mix)

    yssm = _ssm(x, gmix, w_in[:, :o_xbc].T.astype(BF16), w_in[:, o_xbc:o_dt].T.astype(BF16),
                w_in[:, o_dt:o_qkv].T.astype(BF16), lanes(conv_w), lanes(conv_b),
                dt_bias.astype(F32).reshape(-1, 1), (-jnp.exp(a_log.astype(F32))).reshape(-1, 1),
                lanes(jnp.repeat(d_skip, SSM_HEAD_DIM)), lanes(ssm_norm), w_ssm_out.astype(BF16))

    w_qkv = w_in[:, o_qkv:o_gate].reshape(d, 3, ATT_N_GROUPS, ATT_GROUP_DIM)
    w_qkv = w_qkv.transpose(0, 2, 1, 3).reshape(d, 3 * ATT_N_GROUPS * ATT_GROUP_DIM).astype(BF16)
    tabs = [_rope_tables(s, r, PROJ_TOKENS) for _, r in ATT_PATTERNS]
    for window, r in ATT_PATTERNS:
        assert window // r == ATT_BLOCK
    qkv = _proj_tm(x, gmix, w_qkv, tabs)
    att = _attn(qkv)

    return _tail(x, yssm, att, gmix, w_in[:, o_gate:].astype(BF16), row(b_gate),
                 w_att_out.astype(BF16), w_mix_out.astype(BF16), row(norm_ffn),
                 w_ffn_gate.astype(BF16), w_ffn_up.astype(BF16), w_ffn_down.astype(BF16), row(norm_out))


def kernel(x, norm_mix, w_in, b_gate, conv_w, conv_b, dt_bias, a_log, d_skip, ssm_norm, w_ssm_out,
           w_att_out, w_mix_out, norm_ffn, w_ffn_gate, w_ffn_up, w_ffn_down, norm_final):
    depth = w_in.shape[0]
    assert depth == 1, "the tail kernel fuses the final rmsnorm into the last (only) layer"
    return _layer(x, norm_mix[0], w_in[0], b_gate[0], conv_w[0], conv_b[0], dt_bias[0], a_log[0],
                  d_skip[0], ssm_norm[0], w_ssm_out[0], w_att_out[0], w_mix_out[0], norm_ffn[0],
                  w_ffn_gate[0], w_ffn_up[0], w_ffn_down[0], norm_final)
```

```python
import functools
import math

import jax
import jax.numpy as jnp
from jax import lax
from jax.experimental import pallas as pl
from jax.experimental.pallas import tpu as pltpu

D_MODEL = 1024
SSM_D_INNER = 2048
SSM_HEAD_DIM = 64
SSM_N_HEADS = 32
SSM_N_GROUPS = 4
SSM_HEADS_PER_GROUP = SSM_N_HEADS // SSM_N_GROUPS
SSM_GROUP_DIM = SSM_D_INNER // SSM_N_GROUPS
SSM_D_STATE = 128
SSM_CONV = 4
SSM_CHUNK = 128
SSM_BC_DIM = SSM_N_GROUPS * SSM_D_STATE
SSM_CONV_DIM = SSM_D_INNER + 2 * SSM_BC_DIM

ATT_HEAD_DIM = 128
ATT_HEADS_PER_GROUP = 4
ATT_PATTERNS = ((128, 1), (512, 4), (2048, 16))
ATT_N_GROUPS = len(ATT_PATTERNS)
ATT_N_HEADS = ATT_HEADS_PER_GROUP * ATT_N_GROUPS
ATT_GROUP_DIM = ATT_HEADS_PER_GROUP * ATT_HEAD_DIM
ATT_BLOCK = 128
ATT_SPAN = ATT_BLOCK * max(r for _, r in ATT_PATTERNS)
ROPE_THETA = 10000.0

D_FF = 2816
FF_CHUNK = 256
EPS = 1e-6
LOG2E = math.log2(math.e)
LN2 = math.log(2.0)

LANES = 128
VMEM_LIMIT = 56 * 1024 * 1024

PROJ_TOKENS = 512
SSM_TILE_CHUNKS = 4
PROJ_ROWS = 256
TAIL_TOKENS = 512
TAIL_PARTS = 1

F32 = jnp.float32
BF16 = jnp.bfloat16


def _dot(a, b):
    return jnp.dot(a, b, preferred_element_type=F32)


def _dot_nt(a, b):
    return lax.dot_general(a, b, (((1,), (1,)), ((), ())), preferred_element_type=F32)


def _dot_tn(a, b):
    return lax.dot_general(a, b, (((0,), (0,)), ((), ())), preferred_element_type=F32)


def _sigmoid(x):
    return 1.0 / (1.0 + jnp.exp(-x))


def _rmsnorm_rows(x, g):
    return x * lax.rsqrt(jnp.mean(x * x, axis=-1, keepdims=True) + EPS) * g


def _const_spec(shape):
    zeros = (0,) * len(shape)
    return pl.BlockSpec(shape, lambda *_: zeros, pipeline_mode=pl.Buffered(1))


def _proj_tm_kernel(x_ref, g_ref, w_ref, tab0_ref, tab1_ref, tab2_ref, *refs):
    out_refs, (h_s, hp_s) = refs[:3 * ATT_N_GROUPS], refs[3 * ATT_N_GROUPS:]
    tabs = (tab0_ref, tab1_ref, tab2_ref)
    tm = x_ref.shape[1]
    h = _rmsnorm_rows(x_ref[0], g_ref[...])
    n_slabs = D_MODEL // LANES
    for k in range(n_slabs):
        h_s[k] = h[:, k * LANES:(k + 1) * LANES]

    for gi, (_, r) in enumerate(ATT_PATTERNS):
        rows = tm // r
        if r == 1:
            hp = h.astype(BF16)
        else:
            for rho in range(r):
                for k in range(n_slabs):
                    hp_s[rho * rows:(rho + 1) * rows, k * LANES:(k + 1) * LANES] = (
                        h_s[k, pl.ds(rho, rows, stride=r), :].astype(BF16))
            hp = hp_s[...]
        acc = _dot(hp, w_ref[:, gi * 3 * ATT_GROUP_DIM:(gi + 1) * 3 * ATT_GROUP_DIM])
        tab = tabs[gi]
        for kind in range(3):
            out = out_refs[gi * 3 + kind]
            for j in range(ATT_HEADS_PER_GROUP):
                c0 = kind * ATT_GROUP_DIM + j * ATT_HEAD_DIM
                t = acc[:, c0:c0 + ATT_HEAD_DIM]
                if kind < 2:
                    t = t * tab[2 * kind] + pltpu.roll(t, ATT_HEAD_DIM // 2, 1) * tab[2 * kind + 1]
                t = t.astype(BF16)
                for rho in range(r):
                    o0 = (j * r + rho) * ATT_HEAD_DIM
                    out[0, :, o0:o0 + ATT_HEAD_DIM] = t[rho * rows:(rho + 1) * rows, :]


def _proj_tm(x, g, w_qkv, tabs):
    b, s, d = x.shape
    tm = PROJ_TOKENS
    tab_spec = pl.BlockSpec((4, tm, ATT_HEAD_DIM), lambda i, j: (0, j, 0))
    out_specs, out_shape = [], []
    for _, r in ATT_PATTERNS:
        for _ in range(3):
            out_specs.append(pl.BlockSpec((1, tm // r, r * ATT_GROUP_DIM), lambda i, j: (i, j, 0)))
            out_shape.append(jax.ShapeDtypeStruct((b, s // r, r * ATT_GROUP_DIM), BF16))
    return pl.pallas_call(
        _proj_tm_kernel,
        grid=(b, s // tm),
        in_specs=[
            pl.BlockSpec((1, tm, d), lambda i, j: (i, j, 0)),
            _const_spec((1, d)),
            _const_spec(w_qkv.shape),
            tab_spec, tab_spec, tab_spec,
        ],
        out_specs=out_specs,
        out_shape=out_shape,
        scratch_shapes=[
            pltpu.VMEM((d // LANES, tm, LANES), F32),
            pltpu.VMEM((tm, d), BF16),
        ],
        compiler_params=pltpu.CompilerParams(
            dimension_semantics=("parallel", "parallel"), vmem_limit_bytes=VMEM_LIMIT),
        name="proj_tm",
    )(x, g, w_qkv, *tabs)


def _ssm_kernel(x_ref, g_ref, wz_ref, wx_ref, wdt_ref, cw_ref, cb_ref, dtb_ref, a_ref, dsk_ref, nw_ref,
                wout_ref, y_ref, h_s, z_s, x_s, dt_s, zc_s, prev_s, state_s, xbc_s, xdec_s, y_s, *,
                tiles_per_seq):
    g = pl.program_id(0)
    nxt = lax.rem(g, 2)
    cur = 1 - nxt

    @pl.when(g == 0)
    def _():
        z_s[1] = jnp.zeros(z_s.shape[1:], z_s.dtype)
        x_s[1] = jnp.zeros(x_s.shape[1:], x_s.dtype)
        dt_s[1] = jnp.zeros(dt_s.shape[1:], dt_s.dtype)

    @pl.when(lax.rem(jnp.maximum(g - 1, 0), tiles_per_seq) == 0)
    def _():
        prev_s[...] = jnp.zeros_like(prev_s)
        state_s[...] = jnp.zeros_like(state_s)

    h_s[...] = _rmsnorm_rows(x_ref[0], g_ref[...]).astype(BF16)
    dt_all = _dot_nt(wdt_ref[...], h_s[...])
    for k in range(SSM_TILE_CHUNKS):
        dt_s[nxt, k] = dt_all[:, k * SSM_CHUNK:(k + 1) * SSM_CHUNK]

    def chunk(c, carry):
        _conv_chunk(x_s.at[cur, c], cw_ref, cb_ref, prev_s, xbc_s)
        zc_s[...] = z_s[cur, c]
        dt_raw = dt_s[cur, c]

        for w_ref, buf, rows in ((wz_ref, z_s, SSM_D_INNER // SSM_TILE_CHUNKS),
                                 (wx_ref, x_s, SSM_CONV_DIM // SSM_TILE_CHUNKS)):
            for r in range(0, rows, PROJ_ROWS):
                r0 = pl.multiple_of(c * rows + r, PROJ_ROWS)
                part = _dot_nt(w_ref[pl.ds(r0, PROJ_ROWS), :], h_s[...])
                for k in range(SSM_TILE_CHUNKS):
                    buf[nxt, k, pl.ds(r0, PROJ_ROWS), :] = (
                        part[:, k * SSM_CHUNK:(k + 1) * SSM_CHUNK].astype(BF16))

        y_ref[0, pl.ds(pl.multiple_of(c * SSM_CHUNK, SSM_CHUNK), SSM_CHUNK), :] = _ssd_chunk(
            zc_s, dt_raw, dtb_ref, a_ref, dsk_ref, nw_ref, wout_ref, state_s, xbc_s, xdec_s, y_s)
        return carry

    lax.fori_loop(0, SSM_TILE_CHUNKS, chunk, 0)


def _conv_chunk(xbct_ref, cw_ref, cb_ref, prev_s, xbc_s):
    q = SSM_CHUNK
    rb = 256
    lane = lax.broadcasted_iota(jnp.int32, (rb, q), 1)
    for c in range(SSM_CONV_DIM // rb):
        sl = slice(c * rb, (c + 1) * rb)
        cur = xbct_ref[sl, :].astype(F32)
        prev = prev_s[sl, :]
        acc = cb_ref[sl, :] + cw_ref[SSM_CONV - 1, sl, :] * cur
        for s in range(1, SSM_CONV):
            shifted = pltpu.roll(jnp.where(lane >= q - s, prev, cur), s, 1)
            acc = acc + cw_ref[SSM_CONV - 1 - s, sl, :] * shifted
        xbc_s[sl, :] = acc * _sigmoid(acc)
        prev_s[sl, :] = cur


def _ssd_chunk(zt_ref, dt_raw, dtb_ref, a_ref, dsk_ref, nw_ref, wout_ref, state_s, xbc_s, xdec_s, y_s):
    q = SSM_CHUNK

    dt_in = dt_raw + dtb_ref[...]
    dt = jnp.maximum(dt_in, 0.0) + jnp.log(1.0 + jnp.exp(-jnp.abs(dt_in)))
    a = dt * a_ref[...]
    si = lax.broadcasted_iota(jnp.int32, (q, q), 0)
    li = lax.broadcasted_iota(jnp.int32, (q, q), 1)
    upper = (si <= li).astype(BF16)
    a_hi = a.astype(BF16)
    r1 = a - a_hi.astype(F32)
    a_mid = r1.astype(BF16)
    a_lo = (r1 - a_mid.astype(F32)).astype(BF16)
    acs_t = _dot(a_hi, upper) + _dot(a_mid, upper) + _dot(a_lo, upper)
    tot = jnp.broadcast_to(acs_t[:, q - 1:q], (SSM_N_HEADS, q))
    ecs_t = jnp.exp(acs_t)
    dec_t = jnp.exp(tot - acs_t)
    etot = jnp.exp(tot)
    acs2_t = acs_t * LOG2E
    acs2 = acs2_t.T
    causal = li >= si

    bc0 = SSM_D_INNER
    for g in range(SSM_N_GROUPS):
        b_t = xbc_s[bc0 + g * SSM_D_STATE:bc0 + (g + 1) * SSM_D_STATE, :]
        c_t = xbc_s[bc0 + SSM_BC_DIM + g * SSM_D_STATE:bc0 + SSM_BC_DIM + (g + 1) * SSM_D_STATE, :]
        b_g = b_t.T.astype(BF16)
        c_tb = c_t.astype(BF16)
        cb_t = _dot(b_g, c_tb)
        gsl = slice(g * SSM_GROUP_DIM, (g + 1) * SSM_GROUP_DIM)
        y_off = _dot(state_s[gsl, :].astype(BF16), c_tb)
        for j in range(SSM_HEADS_PER_GROUP):
            h = g * SSM_HEADS_PER_GROUP + j
            hsl = slice(h * SSM_HEAD_DIM, (h + 1) * SSM_HEAD_DIM)
            x_h = xbc_s[hsl, :]
            xdt = x_h * dt[h:h + 1, :]
            seg2 = jnp.where(causal, acs2_t[h:h + 1, :] - acs2[:, h:h + 1], -jnp.inf)
            m_t = (cb_t * jnp.exp2(seg2)).astype(BF16)
            y_diag = _dot(xdt.astype(BF16), m_t)
            xdec_s[hsl, :] = (xdt * dec_t[h:h + 1, :]).astype(BF16)
            y_s[hsl, :] = (y_diag + y_off[j * SSM_HEAD_DIM:(j + 1) * SSM_HEAD_DIM, :] * ecs_t[h:h + 1, :]
                           + dsk_ref[hsl, :] * x_h)
        new_states = _dot(xdec_s[gsl, :], b_g)
        for j in range(SSM_HEADS_PER_GROUP):
            h = g * SSM_HEADS_PER_GROUP + j
            hsl = slice(h * SSM_HEAD_DIM, (h + 1) * SSM_HEAD_DIM)
            state_s[hsl, :] = (state_s[hsl, :] * etot[h:h + 1, :]
                               + new_states[j * SSM_HEAD_DIM:(j + 1) * SSM_HEAD_DIM, :])

    for g in range(SSM_N_GROUPS):
        gsl = slice(g * SSM_GROUP_DIM, (g + 1) * SSM_GROUP_DIM)
        z = zt_ref[gsl, :].astype(F32)
        y = y_s[gsl, :] * (z * _sigmoid(z))
        ms = jnp.sum(y * y, axis=0, keepdims=True) * (1.0 / SSM_GROUP_DIM)
        xdec_s[gsl, :] = (y * lax.rsqrt(ms + EPS) * nw_ref[gsl, :]).astype(BF16)
    return _dot_tn(xdec_s[...], wout_ref[...]).astype(BF16)


def _ssm(x, g, wz_t, wx_t, wdt_t, cw, cb, dtb, a_neg, dsk, nw, wout):
    b, s, d = x.shape
    q = SSM_CHUNK
    tile = SSM_TILE_CHUNKS * q
    tiles_per_seq = s // tile
    n_tiles = b * tiles_per_seq

    def proj_tile(i):
        t = jnp.minimum(i, n_tiles - 1)
        return (t // tiles_per_seq, t % tiles_per_seq, 0)

    def scan_tile(i):
        t = jnp.maximum(i - 1, 0)
        return (t // tiles_per_seq, t % tiles_per_seq, 0)

    consts = (g, wz_t, wx_t, wdt_t, cw, cb, dtb, a_neg, dsk, nw, wout)
    return pl.pallas_call(
        functools.partial(_ssm_kernel, tiles_per_seq=tiles_per_seq),
        grid=(n_tiles + 1,),
        in_specs=[pl.BlockSpec((1, tile, d), proj_tile)] + [_const_spec(c.shape) for c in consts],
        out_specs=pl.BlockSpec((1, tile, D_MODEL), scan_tile),
        out_shape=jax.ShapeDtypeStruct((b, s, D_MODEL), BF16),
        scratch_shapes=[
            pltpu.VMEM((tile, d), BF16),
            pltpu.VMEM((2, SSM_TILE_CHUNKS, SSM_D_INNER, q), BF16),
            pltpu.VMEM((2, SSM_TILE_CHUNKS, SSM_CONV_DIM, q), BF16),
            pltpu.VMEM((2, SSM_TILE_CHUNKS, SSM_N_HEADS, q), F32),
            pltpu.VMEM((SSM_D_INNER, q), BF16),
            pltpu.VMEM((SSM_CONV_DIM, q), F32),
            pltpu.VMEM((SSM_D_INNER, q), F32),
            pltpu.VMEM((SSM_CONV_DIM, q), F32),
            pltpu.VMEM((SSM_D_INNER, q), BF16),
            pltpu.VMEM((SSM_D_INNER, q), F32),
        ],
        compiler_params=pltpu.CompilerParams(
            dimension_semantics=("arbitrary",), vmem_limit_bytes=VMEM_LIMIT),
        name="ssm",
    )(x, *consts)


def _attn_kernel(*refs):
    ins = refs[:5 * ATT_N_GROUPS]
    att_ref, o_s, l_s = refs[5 * ATT_N_GROUPS:]
    blk = ATT_BLOCK
    span = pl.program_id(1)

    qi = lax.broadcasted_iota(jnp.int32, (blk, 2 * blk), 0)
    kj = lax.broadcasted_iota(jnp.int32, (blk, 2 * blk), 1)
    dist = qi + blk - kj
    band = (dist >= 0) & (dist <= blk)
    band_first = band & (kj >= jnp.where(span > 0, 0, blk))

    for gi, (_, r) in enumerate(ATT_PATTERNS):
        q_ref, kc_ref, kp_ref, vc_ref, vp_ref = ins[5 * gi:5 * gi + 5]
        for nl in range(ATT_SPAN // (blk * r)):
            for rho in range(r):
                csl = slice(rho * ATT_HEAD_DIM, (rho + 1) * ATT_HEAD_DIM)
                qb = q_ref[0, nl * blk:(nl + 1) * blk, csl]
                if nl == 0:
                    kb = jnp.concatenate([kp_ref[0, :, csl], kc_ref[0, 0:blk, csl]], axis=0)
                    vb = jnp.concatenate([vp_ref[0, :, csl], vc_ref[0, 0:blk, csl]], axis=0)
                else:
                    kb = kc_ref[0, (nl - 1) * blk:(nl + 1) * blk, csl]
                    vb = vc_ref[0, (nl - 1) * blk:(nl + 1) * blk, csl]
                s = _dot_nt(qb, kb)
                s = jnp.where(band_first if nl == 0 else band, s, -jnp.inf)
                m = jnp.max(s, axis=-1, keepdims=True)
                p = jnp.exp2(s - m)
                den = jnp.sum(p, axis=-1, keepdims=True)
                o = _dot(p.astype(BF16), vb) / den
                lse = jnp.broadcast_to(m * LN2 + jnp.log(den), (blk, LANES))
                if r == 1:
                    rows = slice(nl * blk, (nl + 1) * blk)
                else:
                    rows = pl.ds(nl * blk * r + rho, blk, stride=r)
                o_s[gi, rows, :] = o
                l_s[gi, rows, :] = lse

    mrows = 256
    for c in range(ATT_SPAN // mrows):
        rsl = slice(c * mrows, (c + 1) * mrows)
        ls = [l_s[gi, rsl, :] for gi in range(ATT_N_GROUPS)]
        m = functools.reduce(jnp.maximum, ls)
        es = [jnp.exp(l - m) for l in ls]
        num = sum(e * o_s[gi, rsl, :] for gi, e in enumerate(es))
        att_ref[0, rsl, :] = (num / sum(es)).astype(BF16)


def _attn(qkv):
    b = qkv[0].shape[0]
    s = qkv[0].shape[1] * ATT_PATTERNS[0][1]
    in_specs, args = [], []
    for gi, (_, r) in enumerate(ATT_PATTERNS):
        nbl = ATT_SPAN // (ATT_BLOCK * r)
        cols = r * ATT_HEAD_DIM
        cur = pl.BlockSpec((1, nbl * ATT_BLOCK, cols), lambda i, sp, j: (i, sp, j))
        prev = pl.BlockSpec((1, ATT_BLOCK, cols),
                            lambda i, sp, j, nbl=nbl: (i, jnp.maximum(sp * nbl - 1, 0), j))
        q, k, v = qkv[3 * gi:3 * gi + 3]
        in_specs += [cur, cur, prev, cur, prev]
        args += [q, k, k, v, v]
    return pl.pallas_call(
        _attn_kernel,
        grid=(b, s // ATT_SPAN, ATT_HEADS_PER_GROUP),
        in_specs=in_specs,
        out_specs=pl.BlockSpec((1, ATT_SPAN, ATT_HEAD_DIM), lambda i, sp, j: (i, sp, j)),
        out_shape=jax.ShapeDtypeStruct((b, s, ATT_GROUP_DIM), BF16),
        scratch_shapes=[
            pltpu.VMEM((ATT_N_GROUPS, ATT_SPAN, LANES), F32),
            pltpu.VMEM((ATT_N_GROUPS, ATT_SPAN, LANES), F32),
        ],
        compiler_params=pltpu.CompilerParams(
            dimension_semantics=("parallel", "parallel", "parallel"), vmem_limit_bytes=VMEM_LIMIT),
        name="attn",
    )(*args)


def _tail_kernel(x_ref, yssm_ref, att_ref, gmix_ref, wgate_ref, bgate_ref, watt_ref, wmix_ref,
                 gffn_ref, wg_ref, wu_ref, wd_ref, gfin_ref, out_ref):
    rows = x_ref.shape[1] // TAIL_PARTS
    for part in range(TAIL_PARTS):
        rsl = slice(part * rows, (part + 1) * rows)
        x = x_ref[0, rsl, :]
        h = _rmsnorm_rows(x, gmix_ref[...]).astype(BF16)
        gates = _sigmoid(_dot(h, wgate_ref[...]) + bgate_ref[...])
        y_att = _dot(att_ref[0, rsl, :], watt_ref[...])
        mixed = gates[:, :D_MODEL] * yssm_ref[0, rsl, :].astype(F32) + gates[:, D_MODEL:] * y_att
        x1 = x + _dot(mixed.astype(BF16), wmix_ref[...])

        h2 = _rmsnorm_rows(x1, gffn_ref[...]).astype(BF16)
        acc = x1
        for c in range(D_FF // FF_CHUNK):
            csl = slice(c * FF_CHUNK, (c + 1) * FF_CHUNK)
            gate = _dot(h2, wg_ref[:, csl])
            up = _dot(h2, wu_ref[:, csl])
            act = (gate * _sigmoid(gate) * up).astype(BF16)
            acc = acc + _dot(act, wd_ref[csl, :])
        out_ref[0, rsl, :] = _rmsnorm_rows(acc, gfin_ref[...])


def _tail(x, yssm, att, *consts):
    b, s, d = x.shape
    t = TAIL_TOKENS
    tok = lambda w: pl.BlockSpec((1, t, w), lambda i, j: (i, j, 0))
    return pl.pallas_call(
        _tail_kernel,
        grid=(b, s // t),
        in_specs=[tok(d), tok(d), tok(ATT_GROUP_DIM)] + [_const_spec(c.shape) for c in consts],
        out_specs=tok(d),
        out_shape=jax.ShapeDtypeStruct((b, s, d), F32),
        compiler_params=pltpu.CompilerParams(
            dimension_semantics=("parallel", "parallel"), vmem_limit_bytes=VMEM_LIMIT),
        name="tail",
    )(x, yssm, att, *consts)


def _rope_tables(s, r, tm):
    half = ATT_HEAD_DIM // 2
    inv = ROPE_THETA ** (-jnp.arange(half, dtype=F32) / half)
    pos = jnp.arange(s).reshape(s // tm, tm // r, r).transpose(0, 2, 1).reshape(s)
    ang = pos.astype(F32)[:, None] * inv[None, :]
    cos = jnp.cos(ang)
    sin = jnp.sin(ang)
    cos_full = jnp.concatenate([cos, cos], axis=-1)
    sin_signed = jnp.concatenate([-sin, sin], axis=-1)
    scale = ATT_HEAD_DIM ** -0.5 * LOG2E
    return jnp.stack([cos_full * scale, sin_signed * scale, cos_full, sin_signed])


def _layer(x, norm_mix, w_in, b_gate, conv_w, conv_b, dt_bias, a_log, d_skip, ssm_norm,
           w_ssm_out, w_att_out, w_mix_out, norm_ffn, w_ffn_gate, w_ffn_up, w_ffn_down, norm_out):
    b, s, d = x.shape
    assert d == D_MODEL and s % ATT_SPAN == 0
    q = SSM_CHUNK
    o_xbc = SSM_D_INNER
    o_dt = o_xbc + SSM_CONV_DIM
    o_qkv = o_dt + SSM_N_HEADS
    o_gate = o_qkv + 3 * ATT_N_HEADS * ATT_HEAD_DIM

    row = lambda v: v.astype(F32).reshape(1, -1)
    lanes = lambda v: jnp.broadcast_to(v.astype(F32)[..., None], v.shape + (q,))
    gmix = row(norm_mix)

    yssm = _ssm(x, gmix, w_in[:, :o_xbc].T.astype(BF16), w_in[:, o_xbc:o_dt].T.astype(BF16),
                w_in[:, o_dt:o_qkv].T.astype(BF16), lanes(conv_w), lanes(conv_b),
                dt_bias.astype(F32).reshape(-1, 1), (-jnp.exp(a_log.astype(F32))).reshape(-1, 1),
                lanes(jnp.repeat(d_skip, SSM_HEAD_DIM)), lanes(ssm_norm), w_ssm_out.astype(BF16))

    w_qkv = w_in[:, o_qkv:o_gate].reshape(d, 3, ATT_N_GROUPS, ATT_GROUP_DIM)
    w_qkv = w_qkv.transpose(0, 2, 1, 3).reshape(d, 3 * ATT_N_GROUPS * ATT_GROUP_DIM).astype(BF16)
    tabs = [_rope_tables(s, r, PROJ_TOKENS) for _, r in ATT_PATTERNS]
    for window, r in ATT_PATTERNS:
        assert window // r == ATT_BLOCK
    qkv = _proj_tm(x, gmix, w_qkv, tabs)
    att = _attn(qkv)

    return _tail(x, yssm, att, gmix, w_in[:, o_gate:].astype(BF16), row(b_gate),
                 w_att_out.astype(BF16), w_mix_out.astype(BF16), row(norm_ffn),
                 w_ffn_gate.astype(BF16), w_ffn_up.astype(BF16), w_ffn_down.astype(BF16), row(norm_out))


def kernel(x, norm_mix, w_in, b_gate, conv_w, conv_b, dt_bias, a_log, d_skip, ssm_norm, w_ssm_out,
           w_att_out, w_mix_out, norm_ffn, w_ffn_gate, w_ffn_up, w_ffn_down, norm_final):
    depth = w_in.shape[0]
    assert depth == 1, "the tail kernel fuses the final rmsnorm into the last (only) layer"
    return _layer(x, norm_mix[0], w_in[0], b_gate[0], conv_w[0], conv_b[0], dt_bias[0], a_log[0],
                  d_skip[0], ssm_norm[0], w_ssm_out[0], w_att_out[0], w_mix_out[0], norm_ffn[0],
                  w_ffn_gate[0], w_ffn_up[0], w_ffn_down[0], norm_final)
```

```python
import functools
import math

import jax
import jax.numpy as jnp
from jax import lax
from jax.experimental import pallas as pl
from jax.experimental.pallas import tpu as pltpu

D_MODEL = 1024
SSM_D_INNER = 2048
SSM_HEAD_DIM = 64
SSM_N_HEADS = 32
SSM_N_GROUPS = 4
SSM_HEADS_PER_GROUP = SSM_N_HEADS // SSM_N_GROUPS
SSM_GROUP_DIM = SSM_D_INNER // SSM_N_GROUPS
SSM_D_STATE = 128
SSM_CONV = 4
SSM_CHUNK = 128
SSM_BC_DIM = SSM_N_GROUPS * SSM_D_STATE
SSM_CONV_DIM = SSM_D_INNER + 2 * SSM_BC_DIM

ATT_HEAD_DIM = 128
ATT_HEADS_PER_GROUP = 4
ATT_PATTERNS = ((128, 1), (512, 4), (2048, 16))
ATT_N_GROUPS = len(ATT_PATTERNS)
ATT_N_HEADS = ATT_HEADS_PER_GROUP * ATT_N_GROUPS
ATT_GROUP_DIM = ATT_HEADS_PER_GROUP * ATT_HEAD_DIM
ATT_BLOCK = 128
ATT_SPAN = ATT_BLOCK * max(r for _, r in ATT_PATTERNS)
ROPE_THETA = 10000.0

D_FF = 2816
FF_CHUNK = 256
EPS = 1e-6
LOG2E = math.log2(math.e)
LN2 = math.log(2.0)

LANES = 128
VMEM_LIMIT = 56 * 1024 * 1024

PROJ_TOKENS = 512
SSM_TILE_CHUNKS = 4
PROJ_ROWS = 256
TAIL_TOKENS = 512
TAIL_PARTS = 1

F32 = jnp.float32
BF16 = jnp.bfloat16


def _dot(a, b):
    return jnp.dot(a, b, preferred_element_type=F32)


def _dot_nt(a, b):
    return lax.dot_general(a, b, (((1,), (1,)), ((), ())), preferred_element_type=F32)


def _dot_tn(a, b):
    return lax.dot_general(a, b, (((0,), (0,)), ((), ())), preferred_element_type=F32)


def _sigmoid(x):
    return 1.0 / (1.0 + jnp.exp(-x))


def _rmsnorm_rows(x, g):
    return x * lax.rsqrt(jnp.mean(x * x, axis=-1, keepdims=True) + EPS) * g


def _const_spec(shape):
    zeros = (0,) * len(shape)
    return pl.BlockSpec(shape, lambda *_: zeros, pipeline_mode=pl.Buffered(1))


def _proj_tm_kernel(x_ref, g_ref, w_ref, tab0_ref, tab1_ref, tab2_ref, *refs):
    out_refs, (h_s, hp_s) = refs[:3 * ATT_N_GROUPS], refs[3 * ATT_N_GROUPS:]
    tabs = (tab0_ref, tab1_ref, tab2_ref)
    tm = x_ref.shape[1]
    h = _rmsnorm_rows(x_ref[0], g_ref[...])
    n_slabs = D_MODEL // LANES
    for k in range(n_slabs):
        h_s[k] = h[:, k * LANES:(k + 1) * LANES]

    for gi, (_, r) in enumerate(ATT_PATTERNS):
        rows = tm // r
        if r == 1:
            hp = h.astype(BF16)
        else:
            for rho in range(r):
                for k in range(n_slabs):
                    hp_s[rho * rows:(rho + 1) * rows, k * LANES:(k + 1) * LANES] = (
                        h_s[k, pl.ds(rho, rows, stride=r), :].astype(BF16))
            hp = hp_s[...]
        acc = _dot(hp, w_ref[:, gi * 3 * ATT_GROUP_DIM:(gi + 1) * 3 * ATT_GROUP_DIM])
        tab = tabs[gi]
        for kind in range(3):
            out = out_refs[gi * 3 + kind]
            for j in range(ATT_HEADS_PER_GROUP):
                c0 = kind * ATT_GROUP_DIM + j * ATT_HEAD_DIM
                t = acc[:, c0:c0 + ATT_HEAD_DIM]
                if kind < 2:
                    t = t * tab[2 * kind] + pltpu.roll(t, ATT_HEAD_DIM // 2, 1) * tab[2 * kind + 1]
                t = t.astype(BF16)
                for rho in range(r):
                    o0 = (j * r + rho) * ATT_HEAD_DIM
                    out[0, :, o0:o0 + ATT_HEAD_DIM] = t[rho * rows:(rho + 1) * rows, :]


def _proj_tm(x, g, w_qkv, tabs):
    b, s, d = x.shape
    tm = PROJ_TOKENS
    tab_spec = pl.BlockSpec((4, tm, ATT_HEAD_DIM), lambda i, j: (0, j, 0))
    out_specs, out_shape = [], []
    for _, r in ATT_PATTERNS:
        for _ in range(3):
            out_specs.append(pl.BlockSpec((1, tm // r, r * ATT_GROUP_DIM), lambda i, j: (i, j, 0)))
            out_shape.append(jax.ShapeDtypeStruct((b, s // r, r * ATT_GROUP_DIM), BF16))
    return pl.pallas_call(
        _proj_tm_kernel,
        grid=(b, s // tm),
        in_specs=[
            pl.BlockSpec((1, tm, d), lambda i, j: (i, j, 0)),
            _const_spec((1, d)),
            _const_spec(w_qkv.shape),
            tab_spec, tab_spec, tab_spec,
        ],
        out_specs=out_specs,
        out_shape=out_shape,
        scratch_shapes=[
            pltpu.VMEM((d // LANES, tm, LANES), F32),
            pltpu.VMEM((tm, d), BF16),
        ],
        compiler_params=pltpu.CompilerParams(
            dimension_semantics=("parallel", "parallel"), vmem_limit_bytes=VMEM_LIMIT),
        name="proj_tm",
    )(x, g, w_qkv, *tabs)


def _ssm_kernel(x_ref, g_ref, wz_ref, wx_ref, wdt_ref, cw_ref, cb_ref, dtb_ref, a_ref, dsk_ref, nw_ref,
                wout_ref, y_ref, h_s, z_s, x_s, dt_s, zc_s, prev_s, state_s, xbc_s, xdec_s, y_s, *,
                tiles_per_seq):
    g = pl.program_id(0)
    nxt = lax.rem(g, 2)
    cur = 1 - nxt

    @pl.when(g == 0)
    def _():
        z_s[1] = jnp.zeros(z_s.shape[1:], z_s.dtype)
        x_s[1] = jnp.zeros(x_s.shape[1:], x_s.dtype)
        dt_s[1] = jnp.zeros(dt_s.shape[1:], dt_s.dtype)

    @pl.when(lax.rem(jnp.maximum(g - 1, 0), tiles_per_seq) == 0)
    def _():
        prev_s[...] = jnp.zeros_like(prev_s)
        state_s[...] = jnp.zeros_like(state_s)

    h_s[...] = _rmsnorm_rows(x_ref[0], g_ref[...]).astype(BF16)
    dt_all = _dot_nt(wdt_ref[...], h_s[...])
    for k in range(SSM_TILE_CHUNKS):
        dt_s[nxt, k] = dt_all[:, k * SSM_CHUNK:(k + 1) * SSM_CHUNK]

    def chunk(c, carry):
        _conv_chunk(x_s.at[cur, c], cw_ref, cb_ref, prev_s, xbc_s)
        zc_s[...] = z_s[cur, c]
        dt_raw = dt_s[cur, c]

        for w_ref, buf, rows in ((wz_ref, z_s, SSM_D_INNER // SSM_TILE_CHUNKS),
                                 (wx_ref, x_s, SSM_CONV_DIM // SSM_TILE_CHUNKS)):
            for r in range(0, rows, PROJ_ROWS):
                r0 = pl.multiple_of(c * rows + r, PROJ_ROWS)
                part = _dot_nt(w_ref[pl.ds(r0, PROJ_ROWS), :], h_s[...])
                for k in range(SSM_TILE_CHUNKS):
                    buf[nxt, k, pl.ds(r0, PROJ_ROWS), :] = (
                        part[:, k * SSM_CHUNK:(k + 1) * SSM_CHUNK].astype(BF16))

        y_ref[0, pl.ds(pl.multiple_of(c * SSM_CHUNK, SSM_CHUNK), SSM_CHUNK), :] = _ssd_chunk(
            zc_s, dt_raw, dtb_ref, a_ref, dsk_ref, nw_ref, wout_ref, state_s, xbc_s, xdec_s, y_s)
        return carry

    lax.fori_loop(0, SSM_TILE_CHUNKS, chunk, 0)


def _conv_chunk(xbct_ref, cw_ref, cb_ref, prev_s, xbc_s):
    q = SSM_CHUNK
    rb = 256
    lane = lax.broadcasted_iota(jnp.int32, (rb, q), 1)
    for c in range(SSM_CONV_DIM // rb):
        sl = slice(c * rb, (c + 1) * rb)
        cur = xbct_ref[sl, :].astype(F32)
        prev = prev_s[sl, :]
        acc = cb_ref[sl, :] + cw_ref[SSM_CONV - 1, sl, :] * cur
        for s in range(1, SSM_CONV):
            shifted = pltpu.roll(jnp.where(lane >= q - s, prev, cur), s, 1)
            acc = acc + cw_ref[SSM_CONV - 1 - s, sl, :] * shifted
        xbc_s[sl, :] = acc * _sigmoid(acc)
        prev_s[sl, :] = cur


def _ssd_chunk(zt_ref, dt_raw, dtb_ref, a_ref, dsk_ref, nw_ref, wout_ref, state_s, xbc_s, xdec_s, y_s):
    q = SSM_CHUNK

    dt_in = dt_raw + dtb_ref[...]
    dt = jnp.maximum(dt_in, 0.0) + jnp.log(1.0 + jnp.exp(-jnp.abs(dt_in)))
    a = dt * a_ref[...]
    si = lax.broadcasted_iota(jnp.int32, (q, q), 0)
    li = lax.broadcasted_iota(jnp.int32, (q, q), 1)
    upper = (si <= li).astype(BF16)
    a_hi = a.astype(BF16)
    r1 = a - a_hi.astype(F32)
    a_mid = r1.astype(BF16)
    a_lo = (r1 - a_mid.astype(F32)).astype(BF16)
    acs_t = _dot(a_hi, upper) + _dot(a_mid, upper) + _dot(a_lo, upper)
    tot = jnp.broadcast_to(acs_t[:, q - 1:q], (SSM_N_HEADS, q))
    ecs_t = jnp.exp(acs_t)
    dec_t = jnp.exp(tot - acs_t)
    etot = jnp.exp(tot)
    acs2_t = acs_t * LOG2E
    acs2 = acs2_t.T
    causal = li >= si

    bc0 = SSM_D_INNER
    for g in range(SSM_N_GROUPS):
        b_t = xbc_s[bc0 + g * SSM_D_STATE:bc0 + (g + 1) * SSM_D_STATE, :]
        c_t = xbc_s[bc0 + SSM_BC_DIM + g * SSM_D_STATE:bc0 + SSM_BC_DIM + (g + 1) * SSM_D_STATE, :]
        b_g = b_t.T.astype(BF16)
        c_tb = c_t.astype(BF16)
        cb_t = _dot(b_g, c_tb)
        gsl = slice(g * SSM_GROUP_DIM, (g + 1) * SSM_GROUP_DIM)
        y_off = _dot(state_s[gsl, :].astype(BF16), c_tb)
        for j in range(SSM_HEADS_PER_GROUP):
            h = g * SSM_HEADS_PER_GROUP + j
            hsl = slice(h * SSM_HEAD_DIM, (h + 1) * SSM_HEAD_DIM)
            x_h = xbc_s[hsl, :]
            xdt = x_h * dt[h:h + 1, :]
            seg2 = jnp.where(causal, acs2_t[h:h + 1, :] - acs2[:, h:h + 1], -jnp.inf)
            m_t = (cb_t * jnp.exp2(seg2)).astype(BF16)
            y_diag = _dot(xdt.astype(BF16), m_t)
            xdec_s[hsl, :] = (xdt * dec_t[h:h + 1, :]).astype(BF16)
            y_s[hsl, :] = (y_diag + y_off[j * SSM_HEAD_DIM:(j + 1) * SSM_HEAD_DIM, :] * ecs_t[h:h + 1, :]
                           + dsk_ref[hsl, :] * x_h)
        new_states = _dot(xdec_s[gsl, :], b_g)
        for j in range(SSM_HEADS_PER_GROUP):
            h = g * SSM_HEADS_PER_GROUP + j
            hsl = slice(h * SSM_HEAD_DIM, (h + 1) * SSM_HEAD_DIM)
            state_s[hsl, :] = (state_s[hsl, :] * etot[h:h + 1, :]
                               + new_states[j * SSM_HEAD_DIM:(j + 1) * SSM_HEAD_DIM, :])

    for g in range(SSM_N_GROUPS):
        gsl = slice(g * SSM_GROUP_DIM, (g + 1) * SSM_GROUP_DIM)
        z = zt_ref[gsl, :].astype(F32)
        y = y_s[gsl, :] * (z * _sigmoid(z))
        ms = jnp.sum(y * y, axis=0, keepdims=True) * (1.0 / SSM_GROUP_DIM)
        xdec_s[gsl, :] = (y * lax.rsqrt(ms + EPS) * nw_ref[gsl, :]).astype(BF16)
    return _dot_tn(xdec_s[...], wout_ref[...]).astype(BF16)


def _ssm(x, g, wz_t, wx_t, wdt_t, cw, cb, dtb, a_neg, dsk, nw, wout):
    b, s, d = x.shape
    q = SSM_CHUNK
    tile = SSM_TILE_CHUNKS * q
    tiles_per_seq = s // tile
    n_tiles = b * tiles_per_seq

    def proj_tile(i):
        t = jnp.minimum(i, n_tiles - 1)
        return (t // tiles_per_seq, t % tiles_per_seq, 0)

    def scan_tile(i):
        t = jnp.maximum(i - 1, 0)
        return (t // tiles_per_seq, t % tiles_per_seq, 0)

    consts = (g, wz_t, wx_t, wdt_t, cw, cb, dtb, a_neg, dsk, nw, wout)
    return pl.pallas_call(
        functools.partial(_ssm_kernel, tiles_per_seq=tiles_per_seq),
        grid=(n_tiles + 1,),
        in_specs=[pl.BlockSpec((1, tile, d), proj_tile)] + [_const_spec(c.shape) for c in consts],
        out_specs=pl.BlockSpec((1, tile, D_MODEL), scan_tile),
        out_shape=jax.ShapeDtypeStruct((b, s, D_MODEL), BF16),
        scratch_shapes=[
            pltpu.VMEM((tile, d), BF16),
            pltpu.VMEM((2, SSM_TILE_CHUNKS, SSM_D_INNER, q), BF16),
            pltpu.VMEM((2, SSM_TILE_CHUNKS, SSM_CONV_DIM, q), BF16),
            pltpu.VMEM((2, SSM_TILE_CHUNKS, SSM_N_HEADS, q), F32),
            pltpu.VMEM((SSM_D_INNER, q), BF16),
            pltpu.VMEM((SSM_CONV_DIM, q), F32),
            pltpu.VMEM((SSM_D_INNER, q), F32),
            pltpu.VMEM((SSM_CONV_DIM, q), F32),
            pltpu.VMEM((SSM_D_INNER, q), BF16),
            pltpu.VMEM((SSM_D_INNER, q), F32),
        ],
        compiler_params=pltpu.CompilerParams(
            dimension_semantics=("arbitrary",), vmem_limit_bytes=VMEM_LIMIT),
        name="ssm",
    )(x, *consts)


def _attn_kernel(*refs):
    ins = refs[:5 * ATT_N_GROUPS]
    att_ref, o_s, l_s = refs[5 * ATT_N_GROUPS:]
    blk = ATT_BLOCK
    span = pl.program_id(1)

    qi = lax.broadcasted_iota(jnp.int32, (blk, 2 * blk), 0)
    kj = lax.broadcasted_iota(jnp.int32, (blk, 2 * blk), 1)
    dist = qi + blk - kj
    band = (dist >= 0) & (dist <= blk)
    band_first = band & (kj >= jnp.where(span > 0, 0, blk))

    for gi, (_, r) in enumerate(ATT_PATTERNS):
        q_ref, kc_ref, kp_ref, vc_ref, vp_ref = ins[5 * gi:5 * gi + 5]
        for nl in range(ATT_SPAN // (blk * r)):
            for rho in range(r):
                csl = slice(rho * ATT_HEAD_DIM, (rho + 1) * ATT_HEAD_DIM)
                qb = q_ref[0, nl * blk:(nl + 1) * blk, csl]
                if nl == 0:
                    kb = jnp.concatenate([kp_ref[0, :, csl], kc_ref[0, 0:blk, csl]], axis=0)
                    vb = jnp.concatenate([vp_ref[0, :, csl], vc_ref[0, 0:blk, csl]], axis=0)
                else:
                    kb = kc_ref[0, (nl - 1) * blk:(nl + 1) * blk, csl]
                    vb = vc_ref[0, (nl - 1) * blk:(nl + 1) * blk, csl]
                s = _dot_nt(qb, kb)
                s = jnp.where(band_first if nl == 0 else band, s, -jnp.inf)
                m = jnp.max(s, axis=-1, keepdims=True)
                p = jnp.exp2(s - m)
                den = jnp.sum(p, axis=-1, keepdims=True)
                o = _dot(p.astype(BF16), vb) / den
                lse = jnp.broadcast_to(m * LN2 + jnp.log(den), (blk, LANES))
                if r == 1:
                    rows = slice(nl * blk, (nl + 1) * blk)
                else:
                    rows = pl.ds(nl * blk * r + rho, blk, stride=r)
                o_s[gi, rows, :] = o
                l_s[gi, rows, :] = lse

    mrows = 256
    for c in range(ATT_SPAN // mrows):
        rsl = slice(c * mrows, (c + 1) * mrows)
        ls = [l_s[gi, rsl, :] for gi in range(ATT_N_GROUPS)]
        m = functools.reduce(jnp.maximum, ls)
        es = [jnp.exp(l - m) for l in ls]
        num = sum(e * o_s[gi, rsl, :] for gi, e in enumerate(es))
        att_ref[0, rsl, :] = (num / sum(es)).astype(BF16)


def _attn(qkv):
    b = qkv[0].shape[0]
    s = qkv[0].shape[1] * ATT_PATTERNS[0][1]
    in_specs, args = [], []
    for gi, (_, r) in enumerate(ATT_PATTERNS):
        nbl = ATT_SPAN // (ATT_BLOCK * r)
        cols = r * ATT_HEAD_DIM
        cur = pl.BlockSpec((1, nbl * ATT_BLOCK, cols), lambda i, sp, j: (i, sp, j))
        prev = pl.BlockSpec((1, ATT_BLOCK, cols),
                            lambda i, sp, j, nbl=nbl: (i, jnp.maximum(sp * nbl - 1, 0), j))
        q, k, v = qkv[3 * gi:3 * gi + 3]
        in_specs += [cur, cur, prev, cur, prev]
        args += [q, k, k, v, v]
    return pl.pallas_call(
        _attn_kernel,
        grid=(b, s // ATT_SPAN, ATT_HEADS_PER_GROUP),
        in_specs=in_specs,
        out_specs=pl.BlockSpec((1, ATT_SPAN, ATT_HEAD_DIM), lambda i, sp, j: (i, sp, j)),
        out_shape=jax.ShapeDtypeStruct((b, s, ATT_GROUP_DIM), BF16),
        scratch_shapes=[
            pltpu.VMEM((ATT_N_GROUPS, ATT_SPAN, LANES), F32),
            pltpu.VMEM((ATT_N_GROUPS, ATT_SPAN, LANES), F32),
        ],
        compiler_params=pltpu.CompilerParams(
            dimension_semantics=("parallel", "parallel", "parallel"), vmem_limit_bytes=VMEM_LIMIT),
        name="attn",
    )(*args)


def _tail_kernel(x_ref, yssm_ref, att_ref, gmix_ref, wgate_ref, bgate_ref, watt_ref, wmix_ref,
                 gffn_ref, wg_ref, wu_ref, wd_ref, gfin_ref, out_ref):
    rows = x_ref.shape[1] // TAIL_PARTS
    for part in range(TAIL_PARTS):
        rsl = slice(part * rows, (part + 1) * rows)
        x = x_ref[0, rsl, :]
        h = _rmsnorm_rows(x, gmix_ref[...]).astype(BF16)
        gates = _sigmoid(_dot(h, wgate_ref[...]) + bgate_ref[...])
        y_att = _dot(att_ref[0, rsl, :], watt_ref[...])
        mixed = gates[:, :D_MODEL] * yssm_ref[0, rsl, :].astype(F32) + gates[:, D_MODEL:] * y_att
        x1 = x + _dot(mixed.astype(BF16), wmix_ref[...])

        h2 = _rmsnorm_rows(x1, gffn_ref[...]).astype(BF16)
        acc = x1
        for c in range(D_FF // FF_CHUNK):
            csl = slice(c * FF_CHUNK, (c + 1) * FF_CHUNK)
            gate = _dot(h2, wg_ref[:, csl])
            up = _dot(h2, wu_ref[:, csl])
            act = (gate * _sigmoid(gate) * up).astype(BF16)
            acc = acc + _dot(act, wd_ref[csl, :])
        out_ref[0, rsl, :] = _rmsnorm_rows(acc, gfin_ref[...])


def _tail(x, yssm, att, *consts):
    b, s, d = x.shape
    t = TAIL_TOKENS
    tok = lambda w: pl.BlockSpec((1, t, w), lambda i, j: (i, j, 0))
    return pl.pallas_call(
        _tail_kernel,
        grid=(b, s // t),
        in_specs=[tok(d), tok(d), tok(ATT_GROUP_DIM)] + [_const_spec(c.shape) for c in consts],
        out_specs=tok(d),
        out_shape=jax.ShapeDtypeStruct((b, s, d), F32),
        compiler_params=pltpu.CompilerParams(
            dimension_semantics=("parallel", "parallel"), vmem_limit_bytes=VMEM_LIMIT),
        name="tail",
    )(x, yssm, att, *consts)


def _rope_tables(s, r, tm):
    half = ATT_HEAD_DIM // 2
    inv = ROPE_THETA ** (-jnp.arange(half, dtype=F32) / half)
    pos = jnp.arange(s).reshape(s // tm, tm // r, r).transpose(0, 2, 1).reshape(s)
    ang = pos.astype(F32)[:, None] * inv[None, :]
    cos = jnp.cos(ang)
    sin = jnp.sin(ang)
    cos_full = jnp.concatenate([cos, cos], axis=-1)
    sin_signed = jnp.concatenate([-sin, sin], axis=-1)
    scale = ATT_HEAD_DIM ** -0.5 * LOG2E
    return jnp.stack([cos_full * scale, sin_signed * scale, cos_full, sin_signed])


def _layer(x, norm_mix, w_in, b_gate, conv_w, conv_b, dt_bias, a_log, d_skip, ssm_norm,
           w_ssm_out, w_att_out, w_mix_out, norm_ffn, w_ffn_gate, w_ffn_up, w_ffn_down, norm_out):
    b, s, d = x.shape
    assert d == D_MODEL and s % ATT_SPAN == 0
    q = SSM_CHUNK
    o_xbc = SSM_D_INNER
    o_dt = o_xbc + SSM_CONV_DIM
    o_qkv = o_dt + SSM_N_HEADS
    o_gate = o_qkv + 3 * ATT_N_HEADS * ATT_HEAD_DIM

    row = lambda v: v.astype(F32).reshape(1, -1)
    lanes = lambda v: jnp.broadcast_to(v.astype(F32)[..., None], v.shape + (q,))
    gmix = row(norm_mix)

    w_in = w_in.astype(BF16)
    yssm = _ssm(x, gmix, w_in[:, :o_xbc].T, w_in[:, o_xbc:o_dt].T,
                w_in[:, o_dt:o_qkv].T, lanes(conv_w), lanes(conv_b),
                dt_bias.astype(F32).reshape(-1, 1), (-jnp.exp(a_log.astype(F32))).reshape(-1, 1),
                lanes(jnp.repeat(d_skip, SSM_HEAD_DIM)), lanes(ssm_norm), w_ssm_out.astype(BF16))

    w_qkv = w_in[:, o_qkv:o_gate].reshape(d, 3, ATT_N_GROUPS, ATT_GROUP_DIM)
    w_qkv = w_qkv.transpose(0, 2, 1, 3).reshape(d, 3 * ATT_N_GROUPS * ATT_GROUP_DIM).astype(BF16)
    tabs = [_rope_tables(s, r, PROJ_TOKENS) for _, r in ATT_PATTERNS]
    for window, r in ATT_PATTERNS:
        assert window // r == ATT_BLOCK
    qkv = _proj_tm(x, gmix, w_qkv, tabs)
    att = _attn(qkv)

    return _tail(x, yssm, att, gmix, w_in[:, o_gate:].astype(BF16), row(b_gate),
                 w_att_out.astype(BF16), w_mix_out.astype(BF16), row(norm_ffn),
                 w_ffn_gate.astype(BF16), w_ffn_up.astype(BF16), w_ffn_down.astype(BF16), row(norm_out))


def kernel(x, norm_mix, w_in, b_gate, conv_w, conv_b, dt_bias, a_log, d_skip, ssm_norm, w_ssm_out,
           w_att_out, w_mix_out, norm_ffn, w_ffn_gate, w_ffn_up, w_ffn_down, norm_final):
    depth = w_in.shape[0]
    assert depth == 1, "the tail kernel fuses the final rmsnorm into the last (only) layer"
    return _layer(x, norm_mix[0], w_in[0], b_gate[0], conv_w[0], conv_b[0], dt_bias[0], a_log[0],
                  d_skip[0], ssm_norm[0], w_ssm_out[0], w_att_out[0], w_mix_out[0], norm_ffn[0],
                  w_ffn_gate[0], w_ffn_up[0], w_ffn_down[0], norm_final)
```
